```python
import math
import jax, jax.numpy as jnp
from jax import lax
import numpy as np

D_MODEL = 1024
BATCH = 2
SEQ = 8192
DEPTH = 4

N_META = 16
N_A = DEPTH // 2
N_B = DEPTH - N_A
HG_EXPAND = 128
HG_HEADS = D_MODEL // HG_EXPAND
HG_DK = HG_EXPAND
HG_DV = D_MODEL // HG_HEADS
HG_FDIM = HG_HEADS * HG_DK
HG_CHUNK = 64
F_FLOOR = 1e-30
DA_HEADS = D_MODEL // 128
DA_DH = 64
DA_DV = 2 * DA_DH
DA_QK = DA_HEADS * 2 * DA_DH
Q_BLOCK = 128
D_FF = ((8 * D_MODEL // 3) + 255) // 256 * 256
CONV_W = 3
EPS = 1e-5

kernel_name = "yoco_hgrn2_diffattn_convffn_meta"


def rmsnorm(x, g):
    xf = x.astype(jnp.float32)
    y = xf * lax.rsqrt(jnp.mean(xf * xf, axis=-1, keepdims=True) + EPS)
    return y.astype(x.dtype) * g


def hgrn2_lower_bounds(lb_param):
    p = jax.nn.softmax(lb_param.astype(jnp.float32), axis=0)
    return jnp.cumsum(p, axis=0) - p[0]


def gla_chunk_step(S, inp):
    q, k, v, lf = inp
    C = q.shape[2]
    tri = jnp.tril(jnp.ones((C, C), dtype=bool))
    b = jnp.cumsum(lf, axis=2)
    diff = b[:, :, :, None, :] - b[:, :, None, :, :]
    decay = jnp.exp(jnp.where(tri[None, None, :, :, None], diff, -jnp.inf))
    att = jnp.einsum('bhtd,bhsd,bhtsd->bhts', q, k, decay)
    o = jnp.einsum('bhts,bhse->bhte', att, v) + jnp.einsum('bhtd,bhde->bhte', q * jnp.exp(b), S)
    b_last = b[:, :, -1:, :]
    S = jnp.exp(b_last[:, :, 0, :, None]) * S + jnp.einsum('bhsd,bhse->bhde', k * jnp.exp(b_last - b), v)
    return S, o


def hgrn2_mixer(h, w_in, lb, gain_o, w_o):
    B, L, _ = h.shape
    f32 = jnp.float32
    proj = h @ w_in
    q, fz, i, g = jnp.split(proj, [HG_FDIM, 2 * HG_FDIM, 2 * HG_FDIM + D_MODEL], axis=-1)
    q = jax.nn.silu(q.astype(f32))
    fz = fz.astype(f32)
    lb = lb.astype(f32)
    sig = jax.nn.sigmoid(fz)
    f = lb + (1.0 - lb) * sig
    log_f = jnp.log(jnp.maximum(f, F_FLOOR))
    k = (1.0 - lb) * (1.0 - sig)
    v = i.astype(f32)
    pad = (-L) % HG_CHUNK
    n_chunks = (L + pad) // HG_CHUNK

    def to_chunks(t, d):
        t = jnp.pad(t.reshape(B, L, HG_HEADS, d), ((0, 0), (pad, 0), (0, 0), (0, 0)))
        return t.reshape(B, n_chunks, HG_CHUNK, HG_HEADS, d).transpose(1, 0, 3, 2, 4)

    xs = (to_chunks(q, HG_DK), to_chunks(k, HG_DK), to_chunks(v, HG_DV), to_chunks(log_f, HG_DK))
    S0 = jnp.zeros((B, HG_HEADS, HG_DK, HG_DV), f32)
    _, o = lax.scan(gla_chunk_step, S0, xs)
    o = o.transpose(1, 0, 3, 2, 4).reshape(B, n_chunks * HG_CHUNK, HG_HEADS, HG_DV)[:, pad:]
    o = o * lax.rsqrt(jnp.mean(o * o, axis=-1, keepdims=True) + EPS)
    o = o.reshape(B, L, D_MODEL) * gain_o.astype(f32) * jax.nn.silu(g.astype(f32))
    return o.astype(h.dtype) @ w_o


def shared_kv(h, kv_norm, w_kv):
    B, L, _ = h.shape
    pad = (-L) % Q_BLOCK
    kv = rmsnorm(h, kv_norm) @ w_kv
    k = kv[..., :DA_QK].reshape(B, L, DA_HEADS, 2, DA_DH)
    v = kv[..., DA_QK:].reshape(B, L, DA_HEADS, DA_DV)
    k = jnp.pad(k, ((0, 0), (0, pad), (0, 0), (0, 0), (0, 0)))
    v = jnp.pad(v, ((0, 0), (0, pad), (0, 0), (0, 0)))
    return k, v


def diff_attention(h, w_q, k_sh, v_sh, lam_q1, lam_k1, lam_q2, lam_k2, lam_init, sub_gain, w_o):
    B, L, _ = h.shape
    Lp = k_sh.shape[1]
    f32 = jnp.float32
    q = (h @ w_q).reshape(B, L, DA_HEADS, 2, DA_DH) * (DA_DH ** -0.5)
    q = jnp.pad(q, ((0, 0), (0, Lp - L), (0, 0), (0, 0), (0, 0)))
    lam = (jnp.exp(jnp.sum(lam_q1.astype(f32) * lam_k1.astype(f32)))
           - jnp.exp(jnp.sum(lam_q2.astype(f32) * lam_k2.astype(f32))) + lam_init)
    k_pos = jnp.arange(Lp)

    def block(i):
        qb = lax.dynamic_slice_in_dim(q, i * Q_BLOCK, Q_BLOCK, axis=1)
        s = jnp.einsum('bqhcd,bkhcd->bchqk', qb, k_sh).astype(f32)
        q_pos = i * Q_BLOCK + jnp.arange(Q_BLOCK)
        mask = q_pos[:, None] >= k_pos[None, :]
        p = jax.nn.softmax(jnp.where(mask, s, -jnp.inf), axis=-1)
        a = p[:, 0] - lam * p[:, 1]
        return jnp.einsum('bhqk,bkhe->bqhe', a.astype(v_sh.dtype), v_sh)

    o = lax.map(block, jnp.arange(Lp // Q_BLOCK))
    o = o.transpose(1, 0, 2, 3, 4).reshape(B, Lp, DA_HEADS, DA_DV)[:, :L].astype(f32)
    o = o * lax.rsqrt(jnp.mean(o * o, axis=-1, keepdims=True) + EPS) * sub_gain.astype(f32)
    o = o * (1.0 - lam_init)
    return o.reshape(B, L, D_MODEL).astype(h.dtype) @ w_o


def conv_ffn(h, w_up, conv_w, conv_b, w_down):
    u = h @ w_up
    u = lax.conv_general_dilated(u, conv_w[:, None, :], window_strides=(1,),
                                 padding=[(CONV_W - 1, 0)],
                                 dimension_numbers=('NWC', 'WIO', 'NWC'),
                                 feature_group_count=2 * D_FF) + conv_b
    a, b = jnp.split(u, 2, axis=-1)
    return (jax.nn.silu(a) * b) @ w_down


def setup_inputs(seed: int = 0) -> dict:
    key = jax.random.key(seed)
    ks = jax.random.split(key, 24)
    nrm = lambda k, shape, s: jax.random.normal(k, shape, jnp.float32) * s
    D = D_MODEL
    return {
        "x": nrm(ks[0], (BATCH, SEQ, D), 1.0),
        "meta_tokens": nrm(ks[1], (N_META, D), 1.0),
        "norm_mix": 1.0 + nrm(ks[2], (DEPTH, D), 0.02),
        "norm_ffn": 1.0 + nrm(ks[3], (DEPTH, D), 0.02),
        "hg_w_in": nrm(ks[4], (N_A, D, 2 * HG_FDIM + 2 * D), D ** -0.5),
        "hg_lb": nrm(ks[5], (N_A, HG_FDIM), 0.5),
        "hg_gain_o": 1.0 + nrm(ks[6], (N_A, D), 0.02),
        "hg_w_o": nrm(ks[7], (N_A, D, D), D ** -0.5),
        "kv_norm": 1.0 + nrm(ks[8], (D,), 0.02),
        "w_kv": nrm(ks[9], (D, DA_QK + DA_HEADS * DA_DV), D ** -0.5),
        "da_w_q": nrm(ks[10], (N_B, D, DA_QK), D ** -0.5),
        "da_lam_q1": nrm(ks[11], (N_B, DA_DH), 0.1),
        "da_lam_k1": nrm(ks[12], (N_B, DA_DH), 0.1),
        "da_lam_q2": nrm(ks[13], (N_B, DA_DH), 0.1),
        "da_lam_k2": nrm(ks[14], (N_B, DA_DH), 0.1),
        "da_sub_gain": 1.0 + nrm(ks[15], (N_B, DA_DV), 0.02),
        "da_w_o": nrm(ks[16], (N_B, D, D), D ** -0.5),
        "ffn_w_up": nrm(ks[17], (DEPTH, D, 2 * D_FF), D ** -0.5),
        "ffn_conv_w": nrm(ks[18], (DEPTH, CONV_W, 2 * D_FF), CONV_W ** -0.5),
        "ffn_conv_b": nrm(ks[19], (DEPTH, 2 * D_FF), 0.02),
        "ffn_w_down": nrm(ks[20], (DEPTH, D_FF, D), D_FF ** -0.5),
        "final_norm": 1.0 + nrm(ks[21], (D,), 0.02),
    }


def reference(x, meta_tokens, norm_mix, norm_ffn, hg_w_in, hg_lb, hg_gain_o, hg_w_o,
              kv_norm, w_kv, da_w_q, da_lam_q1, da_lam_k1, da_lam_q2, da_lam_k2,
              da_sub_gain, da_w_o, ffn_w_up, ffn_conv_w, ffn_conv_b, ffn_w_down, final_norm):
    B = x.shape[0]
    meta = jnp.broadcast_to(meta_tokens[None].astype(x.dtype), (B, N_META, D_MODEL))
    h = jnp.concatenate([meta, x], axis=1)
    lbs = hgrn2_lower_bounds(hg_lb)
    k_sh, v_sh = None, None
    for l in range(DEPTH):
        hn = rmsnorm(h, norm_mix[l])
        if l < N_A:
            h = h + hgrn2_mixer(hn, hg_w_in[l], lbs[l], hg_gain_o[l], hg_w_o[l])
        else:
            j = l - N_A
            lam_init = 0.8 - 0.6 * math.exp(-0.3 * l)
            h = h + diff_attention(hn, da_w_q[j], k_sh, v_sh, da_lam_q1[j], da_lam_k1[j],
                                   da_lam_q2[j], da_lam_k2[j], lam_init, da_sub_gain[j], da_w_o[j])
        h = h + conv_ffn(rmsnorm(h, norm_ffn[l]), ffn_w_up[l], ffn_conv_w[l], ffn_conv_b[l], ffn_w_down[l])
        if l == N_A - 1:
            k_sh, v_sh = shared_kv(h, kv_norm, w_kv)
    return rmsnorm(h, final_norm)[:, N_META:]
```

```python
import functools
import math

import jax
import jax.numpy as jnp
from jax import lax
from jax.experimental import pallas as pl
from jax.experimental.pallas import tpu as pltpu

D_MODEL = 1024
N_META = 16
HG_HEADS = 8
HG_DK = 128
HG_FDIM = HG_HEADS * HG_DK
F_FLOOR = 1e-30
DA_HEADS = 8
DA_DH = 64
DA_DV = 128
DA_QK = DA_HEADS * 2 * DA_DH
D_FF = 2816
CONV_W = 3
EPS = 1e-5

LANES = 128
ROW_TILE = 640
L_ALIGN = ROW_TILE
GLA_CHUNK = 64
GLA_SUB = 16
FFN_COL_TILE = 256
PROJ_COL_TILE = 1024
VMEM_LIMIT = 56 * 1024 * 1024
NEG_BIG = -1e30

BF16 = jnp.bfloat16
F32 = jnp.float32


def _params(*sem):
    return pltpu.CompilerParams(dimension_semantics=sem, vmem_limit_bytes=VMEM_LIMIT)


def _sigmoid(x):
    return 1.0 / (1.0 + jnp.exp(-x))


def _rms_rows(x, g):
    return x * lax.rsqrt(jnp.mean(x * x, axis=-1, keepdims=True) + EPS) * g


def _norm_matmul_kernel(x_ref, g_ref, w_ref, o_ref, xn_ref):
    @pl.when(pl.program_id(1) == 0)
    def _():
        xn_ref[...] = _rms_rows(x_ref[...], g_ref[...]).astype(BF16)

    o_ref[...] = jnp.dot(xn_ref[...], w_ref[...], preferred_element_type=F32).astype(o_ref.dtype)


def norm_matmul(x, g, w, out_dtype):
    rows, d = x.shape
    n = w.shape[1]
    tn = min(n, PROJ_COL_TILE)
    return pl.pallas_call(
        _norm_matmul_kernel,
        grid=(rows // ROW_TILE, n // tn),
        in_specs=[
            pl.BlockSpec((ROW_TILE, d), lambda i, j: (i, 0)),
            pl.BlockSpec((1, d), lambda i, j: (0, 0)),
            pl.BlockSpec((d, tn), lambda i, j: (0, j)),
        ],
        out_specs=pl.BlockSpec((ROW_TILE, tn), lambda i, j: (i, j)),
        out_shape=jax.ShapeDtypeStruct((rows, n), out_dtype),
        scratch_shapes=[pltpu.VMEM((ROW_TILE, d), BF16)],
        compiler_params=_params("arbitrary", "arbitrary"),
        name="norm_matmul",
    )(x, g.reshape(1, d), w)


def _matmul_residual_kernel(a_ref, w_ref, h_ref, o_ref):
    o_ref[...] = h_ref[...] + jnp.dot(a_ref[...], w_ref[...], preferred_element_type=F32)


def matmul_residual(a, w, h):
    rows, k = a.shape
    n = w.shape[1]
    return pl.pallas_call(
        _matmul_residual_kernel,
        grid=(rows // ROW_TILE,),
        in_specs=[
            pl.BlockSpec((ROW_TILE, k), lambda i: (i, 0)),
            pl.BlockSpec((k, n), lambda i: (0, 0)),
            pl.BlockSpec((ROW_TILE, n), lambda i: (i, 0)),
        ],
        out_specs=pl.BlockSpec((ROW_TILE, n), lambda i: (i, 0)),
        out_shape=jax.ShapeDtypeStruct((rows, n), F32),
        compiler_params=_params("arbitrary"),
        name="matmul_residual",
    )(a, w, h)


def _gla_kernel(q_ref, f_ref, i_ref, g_ref, lb_ref, gain_ref, o_ref,
                st_ref, b_scr, k_scr, *, layer, n_chunks):
    C, SUB = GLA_CHUNK, GLA_SUB
    n_sub = C // SUB

    @pl.when(pl.program_id(2) == 0)
    def _():
        st_ref[...] = jnp.zeros_like(st_ref)

    lbp = lb_ref[...]
    e = jnp.exp(lbp - jnp.max(lbp, axis=0, keepdims=True))
    p = e / jnp.sum(e, axis=0, keepdims=True)
    lb = jnp.sum(p[:layer + 1], axis=0, keepdims=True) - p[0:1]
    one_m_lb = 1.0 - lb
    gain = gain_ref[...]

    row = lax.broadcasted_iota(jnp.int32, (C, C), 0)
    col = lax.broadcasted_iota(jnp.int32, (C, C), 1)
    tril = (row >= col).astype(BF16)
    sub_row = lax.broadcasted_iota(jnp.int32, (SUB, 1), 0)
    sub_col = lax.broadcasted_iota(jnp.int32, (SUB, C), 1)

    def chunk(c, carry):
        r0 = pl.multiple_of(c * C, C)
        q = q_ref[pl.ds(r0, C), :]
        q = q * _sigmoid(q)
        sig = _sigmoid(f_ref[pl.ds(r0, C), :])
        f = lb + one_m_lb * sig
        lf = jnp.log(jnp.maximum(f, F_FLOOR))
        k = one_m_lb * (1.0 - sig)
        v = i_ref[pl.ds(r0, C), :]
        v16 = v.astype(BF16)

        lf_hi = lf.astype(BF16)
        rem = lf - lf_hi.astype(F32)
        lf_mid = rem.astype(BF16)
        lf_lo = (rem - lf_mid.astype(F32)).astype(BF16)
        b = (jnp.dot(tril, lf_hi, preferred_element_type=F32)
             + jnp.dot(tril, lf_mid, preferred_element_type=F32)
             + jnp.dot(tril, lf_lo, preferred_element_type=F32))
        b_scr[...] = b
        k_scr[...] = k
        b_last = b_scr[C - 1:C, :]

        st = st_ref[...]
        qb = (q * jnp.exp(b)).astype(BF16)
        o_inter = lax.dot_general(qb, st.astype(BF16), (((1,), (1,)), ((), ())),
                                  preferred_element_type=F32)

        a_rows = [jnp.zeros((SUB, C), F32)]
        o_diag = []
        for i in range(n_sub):
            lo = i * SUB
            q_i = q[lo:lo + SUB]
            b_i = b[lo:lo + SUB]
            if i > 0:
                r_i = b_scr[lo - 1:lo, :]
                q_s = (q_i * jnp.exp(b_i - r_i)).astype(BF16)
                k_s = (k * jnp.exp(jnp.minimum(r_i - b, 0.0))).astype(BF16)
                a_i = lax.dot_general(q_s, k_s, (((1,), (1,)), ((), ())),
                                      preferred_element_type=F32)
                a_rows.append(jnp.where(sub_col < lo, a_i, 0.0))
            acc = jnp.zeros((SUB, HG_DK), F32)
            for s in range(SUB):
                b_s = b_scr[lo + s:lo + s + 1, :]
                k_row = k_scr[lo + s:lo + s + 1, :]
                v_row = i_ref[pl.ds(r0 + lo + s, 1), :]
                w = jnp.sum(q_i * k_row * jnp.exp(b_i - b_s), axis=-1, keepdims=True)
                acc = acc + jnp.where(sub_row >= s, w, 0.0) * v_row
            o_diag.append(acc)

        a_off = jnp.concatenate(a_rows, axis=0).astype(BF16)
        o = (o_inter + jnp.dot(a_off, v16, preferred_element_type=F32)
             + jnp.concatenate(o_diag, axis=0))

        kd = (k * jnp.exp(b_last - b)).astype(BF16)
        st_ref[...] = st * jnp.exp(b_last) + lax.dot_general(
            v16, kd, (((0,), (0,)), ((), ())), preferred_element_type=F32)

        g = g_ref[pl.ds(r0, C), :]
        o = o * lax.rsqrt(jnp.mean(o * o, axis=-1, keepdims=True) + EPS)
        o_ref[pl.ds(r0, C), :] = (o * gain * (g * _sigmoid(g))).astype(o_ref.dtype)
        return carry

    lax.fori_loop(0, n_chunks, chunk, 0)


def gla_mixer(proj, hg_lb, gain, layer, batch, l_pad):
    rows = proj.shape[0]
    n_t = l_pad // ROW_TILE
    n_layers = hg_lb.shape[0]
    col = lambda off: (lambda b, h, t: (b * n_t + t, off + h))
    return pl.pallas_call(
        functools.partial(_gla_kernel, layer=layer, n_chunks=ROW_TILE // GLA_CHUNK),
        grid=(batch, HG_HEADS, n_t),
        in_specs=[
            pl.BlockSpec((ROW_TILE, HG_DK), col(0)),
            pl.BlockSpec((ROW_TILE, HG_DK), col(HG_HEADS)),
            pl.BlockSpec((ROW_TILE, HG_DK), col(2 * HG_HEADS)),
            pl.BlockSpec((ROW_TILE, HG_DK), col(3 * HG_HEADS)),
            pl.BlockSpec((n_layers, HG_DK), lambda b, h, t: (0, h)),
            pl.BlockSpec((1, HG_DK), lambda b, h, t: (0, h)),
        ],
        out_specs=pl.BlockSpec((ROW_TILE, HG_DK), col(0)),
        out_shape=jax.ShapeDtypeStruct((rows, D_MODEL), BF16),
        scratch_shapes=[
            pltpu.VMEM((HG_DK, HG_DK), F32),
            pltpu.VMEM((GLA_CHUNK, HG_DK), F32),
            pltpu.VMEM((GLA_CHUNK, HG_DK), F32),
        ],
        compiler_params=_params("arbitrary", "arbitrary", "arbitrary"),
        name="gla_mixer",
    )(proj, proj, proj, proj, hg_lb, gain.reshape(1, D_MODEL))


def _ffn_up_kernel(x_ref, g_ref, wa_ref, wb_ref, cwa_ref, cwb_ref, cba_ref, cbb_ref,
                   o_ref, xn_ref, carry_ref, *, tiles_per_seq):
    i = pl.program_id(0)
    j = pl.program_id(1)

    @pl.when(j == 0)
    def _():
        xn_ref[...] = _rms_rows(x_ref[...], g_ref[...]).astype(BF16)

    xn = xn_ref[...]
    tm = xn.shape[0]
    row = lax.broadcasted_iota(jnp.int32, (tm, 1), 0)

    @pl.when((i % tiles_per_seq) == 0)
    def _():
        carry_ref[j] = jnp.zeros(carry_ref.shape[1:], F32)

    def conv(u, cw_ref, cb_ref, slot):
        tail = carry_ref[j, slot]
        carry_ref[j, slot] = u[tm - 8:, :]
        u1 = jnp.where(row >= 1, pltpu.roll(u, 1, axis=0), tail[7:8, :])
        u2 = jnp.where(row >= 2, pltpu.roll(u, 2, axis=0),
                       jnp.where(row == 0, tail[6:7, :], tail[7:8, :]))
        return (u2 * cw_ref[0:1, :] + u1 * cw_ref[1:2, :] + u * cw_ref[2:3, :]
                + cb_ref[...])

    ua = jnp.dot(xn, wa_ref[...], preferred_element_type=F32)
    ub = jnp.dot(xn, wb_ref[...], preferred_element_type=F32)
    a = conv(ua, cwa_ref, cba_ref, 0)
    b = conv(ub, cwb_ref, cbb_ref, 1)
    o_ref[...] = (a * _sigmoid(a) * b).astype(o_ref.dtype)


def ffn_up(h, g, w_up, conv_w, conv_b, l_pad):
    rows, d = h.shape
    tn = FFN_COL_TILE
    nj = D_FF // tn
    return pl.pallas_call(
        functools.partial(_ffn_up_kernel, tiles_per_seq=l_pad // ROW_TILE),
        grid=(rows // ROW_TILE, nj),
        in_specs=[
            pl.BlockSpec((ROW_TILE, d), lambda i, j: (i, 0)),
            pl.BlockSpec((1, d), lambda i, j: (0, 0)),
            pl.BlockSpec((d, tn), lambda i, j: (0, j)),
            pl.BlockSpec((d, tn), lambda i, j: (0, j + nj)),
            pl.BlockSpec((CONV_W, tn), lambda i, j: (0, j)),
            pl.BlockSpec((CONV_W, tn), lambda i, j: (0, j + nj)),
            pl.BlockSpec((1, tn), lambda i, j: (0, j)),
            pl.BlockSpec((1, tn), lambda i, j: (0, j + nj)),
        ],
        out_specs=pl.BlockSpec((ROW_TILE, tn), lambda i, j: (i, j)),
        out_shape=jax.ShapeDtypeStruct((rows, D_FF), BF16),
        scratch_shapes=[
            pltpu.VMEM((ROW_TILE, d), BF16),
            pltpu.VMEM((nj, 2, 8, tn), F32),
        ],
        compiler_params=_params("arbitrary", "arbitrary"),
        name="ffn_up",
    )(h, g.reshape(1, d), w_up, w_up, conv_w, conv_w,
      conv_b.reshape(1, 2 * D_FF), conv_b.reshape(1, 2 * D_FF))


def _diff_attn_kernel(q_ref, k_ref, v_ref, lq1_ref, lk1_ref, lq2_ref, lk2_ref, gain_ref,
                      o_ref, qs_ref, m_ref, l_ref, acc_ref, *, lam_init):
    i = pl.program_id(2)
    j = pl.program_id(3)
    tq = q_ref.shape[0]
    tk = k_ref.shape[0]

    @pl.when(j == 0)
    def _():
        q = q_ref[...] * (DA_DH ** -0.5)
        lane = lax.broadcasted_iota(jnp.int32, q.shape, 1)
        qs_ref[0:tq, :] = jnp.where(lane < DA_DH, q, 0.0).astype(BF16)
        qs_ref[tq:, :] = jnp.where(lane >= DA_DH, q, 0.0).astype(BF16)
        m_ref[...] = jnp.full_like(m_ref, NEG_BIG)
        l_ref[...] = jnp.zeros_like(l_ref)
        acc_ref[...] = jnp.zeros_like(acc_ref)

    def step(masked):
        s = lax.dot_general(qs_ref[...], k_ref[...], (((1,), (1,)), ((), ())),
                            preferred_element_type=F32)
        if masked:
            r = lax.broadcasted_iota(jnp.int32, (2 * tq, tk), 0)
            c = lax.broadcasted_iota(jnp.int32, (2 * tq, tk), 1)
            r = jnp.where(r >= tq, r - tq, r)
            s = jnp.where(r >= c, s, NEG_BIG)
        m_prev = m_ref[...]
        m_new = jnp.maximum(m_prev, jnp.max(s, axis=-1, keepdims=True))
        alpha = jnp.exp(m_prev - m_new)
        p = jnp.exp(s - m_new)
        l_ref[...] = alpha * l_ref[...] + jnp.sum(p, axis=-1, keepdims=True)
        acc_ref[...] = alpha * acc_ref[...] + jnp.dot(
            p.astype(BF16), v_ref[...], preferred_element_type=F32)
        m_ref[...] = m_new

    @pl.when(j < i)
    def _():
        step(False)

    @pl.when(j == i)
    def _():
        step(True)
        lam = (jnp.exp(jnp.sum(lq1_ref[...] * lk1_ref[...], axis=-1, keepdims=True))
               - jnp.exp(jnp.sum(lq2_ref[...] * lk2_ref[...], axis=-1, keepdims=True))
               + lam_init)
        acc = acc_ref[...]
        l = l_ref[...]
        o = acc[0:tq] / l[0:tq] - lam * (acc[tq:] / l[tq:])
        o = o * lax.rsqrt(jnp.mean(o * o, axis=-1, keepdims=True) + EPS) * gain_ref[...]
        o_ref[...] = (o * (1.0 - lam_init)).astype(o_ref.dtype)


def diff_attention_core(q, kv, lq1, lk1, lq2, lk2, sub_gain, lam_init, batch, l_pad):
    rows = q.shape[0]
    t = ROW_TILE
    n_t = l_pad // t
    vec = lambda a: a.reshape(1, -1)
    small = lambda n: pl.BlockSpec((1, n), lambda b, h, i, j: (0, 0))
    return pl.pallas_call(
        functools.partial(_diff_attn_kernel, lam_init=lam_init),
        grid=(batch, DA_HEADS, n_t, n_t),
        in_specs=[
            pl.BlockSpec((t, 2 * DA_DH), lambda b, h, i, j: (b * n_t + i, h)),
            pl.BlockSpec((t, 2 * DA_DH), lambda b, h, i, j: (b * n_t + jnp.minimum(i, j), h)),
            pl.BlockSpec((t, DA_DV),
                         lambda b, h, i, j: (b * n_t + jnp.minimum(i, j), DA_HEADS + h)),
            small(DA_DH), small(DA_DH), small(DA_DH), small(DA_DH), small(DA_DV),
        ],
        out_specs=pl.BlockSpec((t, DA_DV), lambda b, h, i, j: (b * n_t + i, h)),
        out_shape=jax.ShapeDtypeStruct((rows, D_MODEL), BF16),
        scratch_shapes=[
            pltpu.VMEM((2 * t, 2 * DA_DH), BF16),
            pltpu.VMEM((2 * t, 1), F32),
            pltpu.VMEM((2 * t, 1), F32),
            pltpu.VMEM((2 * t, DA_DV), F32),
        ],
        compiler_params=_params("arbitrary", "arbitrary", "arbitrary", "arbitrary"),
        name="diff_attention",
    )(q, kv, kv, vec(lq1), vec(lk1), vec(lq2), vec(lk2), vec(sub_gain))


def _final_norm_kernel(x_ref, g_ref, o_ref):
    o_ref[...] = _rms_rows(x_ref[...], g_ref[...])


def final_norm(h, g):
    rows, d = h.shape
    return pl.pallas_call(
        _final_norm_kernel,
        grid=(rows // ROW_TILE,),
        in_specs=[pl.BlockSpec((ROW_TILE, d), lambda i: (i, 0)),
                  pl.BlockSpec((1, d), lambda i: (0, 0))],
        out_specs=pl.BlockSpec((ROW_TILE, d), lambda i: (i, 0)),
        out_shape=jax.ShapeDtypeStruct((rows, d), F32),
        compiler_params=_params("arbitrary"),
        name="final_norm",
    )(h, g.reshape(1, d))


def kernel(x, meta_tokens, norm_mix, norm_ffn, hg_w_in, hg_lb, hg_gain_o, hg_w_o, kv_norm, w_kv, da_w_q, da_lam_q1, da_lam_k1, da_lam_q2, da_lam_k2, da_sub_gain, da_w_o, ffn_w_up, ffn_conv_w, ffn_conv_b, ffn_w_down, final_norm_g):
    batch, seq, d = x.shape
    n_a = hg_w_in.shape[0]
    depth = norm_mix.shape[0]
    l_real = N_META + seq
    l_pad = -(-l_real // L_ALIGN) * L_ALIGN

    meta = jnp.broadcast_to(meta_tokens[None].astype(x.dtype), (batch, N_META, d))
    tail = jnp.zeros((batch, l_pad - l_real, d), x.dtype)
    h = jnp.concatenate([meta, x, tail], axis=1).reshape(batch * l_pad, d)

    kv = None
    for l in range(depth):
        if l < n_a:
            proj = norm_matmul(h, norm_mix[l], hg_w_in[l].astype(BF16), F32)
            o = gla_mixer(proj, hg_lb, hg_gain_o[l], l, batch, l_pad)
            h = matmul_residual(o, hg_w_o[l].astype(BF16), h)
        else:
            j = l - n_a
            lam_init = 0.8 - 0.6 * math.exp(-0.3 * l)
            q = norm_matmul(h, norm_mix[l], da_w_q[j].astype(BF16), BF16)
            o = diff_attention_core(q, kv, da_lam_q1[j], da_lam_k1[j], da_lam_q2[j],
                                    da_lam_k2[j], da_sub_gain[j], lam_init, batch, l_pad)
            h = matmul_residual(o, da_w_o[j].astype(BF16), h)
        gated = ffn_up(h, norm_ffn[l], ffn_w_up[l].astype(BF16), ffn_conv_w[l],
                       ffn_conv_b[l], l_pad)
        h = matmul_residual(gated, ffn_w_down[l].astype(BF16), h)
        if l == n_a - 1:
            kv = norm_matmul(h, kv_norm, w_kv.astype(BF16), BF16)

    out = final_norm(h, final_norm_g).reshape(batch, l_pad, d)
    return out[:, N_META:l_real]
```

```python
import functools
import math

import jax
import jax.numpy as jnp
from jax import lax
from jax.experimental import pallas as pl
from jax.experimental.pallas import tpu as pltpu

D_MODEL = 1024
N_META = 16
HG_HEADS = 8
HG_DK = 128
HG_FDIM = HG_HEADS * HG_DK
F_FLOOR = 1e-30
DA_HEADS = 8
DA_DH = 64
DA_DV = 128
DA_QK = DA_HEADS * 2 * DA_DH
D_FF = 2816
CONV_W = 3
EPS = 1e-5

LANES = 128
MXU_DIM = 256
ROW_TILE = 3 * MXU_DIM
L_ALIGN = ROW_TILE
ATT_SUB = MXU_DIM
LOG2E = 1.4426950408889634
GLA_CHUNK = 64
GLA_SUB = 16
FFN_COL_TILE = 256
PROJ_COL_TILE = 1024
VMEM_LIMIT = 56 * 1024 * 1024
NEG_BIG = -1e30

BF16 = jnp.bfloat16
F32 = jnp.float32


def _params(*sem):
    return pltpu.CompilerParams(dimension_semantics=sem, vmem_limit_bytes=VMEM_LIMIT)


def _sigmoid(x):
    return 1.0 / (1.0 + jnp.exp(-x))


def _rms_rows(x, g):
    return x * lax.rsqrt(jnp.mean(x * x, axis=-1, keepdims=True) + EPS) * g


def _norm_matmul_kernel(x_ref, g_ref, w_ref, o_ref, xn_ref, *, scale):
    @pl.when(pl.program_id(1) == 0)
    def _():
        xn_ref[...] = _rms_rows(x_ref[...], g_ref[...]).astype(BF16)

    y = jnp.dot(xn_ref[...], w_ref[...], preferred_element_type=F32)
    if scale != 1.0:
        y = y * scale
    o_ref[...] = y.astype(o_ref.dtype)


def norm_matmul(x, g, w, out_dtype, scale=1.0):
    rows, d = x.shape
    n = w.shape[1]
    tn = min(n, PROJ_COL_TILE)
    return pl.pallas_call(
        functools.partial(_norm_matmul_kernel, scale=scale),
        grid=(rows // ROW_TILE, n // tn),
        in_specs=[
            pl.BlockSpec((ROW_TILE, d), lambda i, j: (i, 0)),
            pl.BlockSpec((1, d), lambda i, j: (0, 0)),
            pl.BlockSpec((d, tn), lambda i, j: (0, j)),
        ],
        out_specs=pl.BlockSpec((ROW_TILE, tn), lambda i, j: (i, j)),
        out_shape=jax.ShapeDtypeStruct((rows, n), out_dtype),
        scratch_shapes=[pltpu.VMEM((ROW_TILE, d), BF16)],
        compiler_params=_params("arbitrary", "arbitrary"),
        name="norm_matmul",
    )(x, g.reshape(1, d), w)


def _norm_matmul_t_kernel(x_ref, g_ref, wt_ref, o_ref):
    xn = _rms_rows(x_ref[...], g_ref[...]).astype(BF16)
    o_ref[...] = lax.dot_general(wt_ref[...], xn, (((1,), (1,)), ((), ())),
                                 preferred_element_type=F32).astype(o_ref.dtype)


def norm_matmul_t(x, g, wt, out_dtype):
    rows, d = x.shape
    n = wt.shape[0]
    return pl.pallas_call(
        _norm_matmul_t_kernel,
        grid=(rows // ROW_TILE,),
        in_specs=[
            pl.BlockSpec((ROW_TILE, d), lambda i: (i, 0)),
            pl.BlockSpec((1, d), lambda i: (0, 0)),
            pl.BlockSpec((n, d), lambda i: (0, 0)),
        ],
        out_specs=pl.BlockSpec((n, ROW_TILE), lambda i: (0, i)),
        out_shape=jax.ShapeDtypeStruct((n, rows), out_dtype),
        compiler_params=_params("arbitrary"),
        name="norm_matmul_t",
    )(x, g.reshape(1, d), wt)


def _matmul_residual_kernel(a_ref, w_ref, h_ref, o_ref):
    o_ref[...] = h_ref[...] + jnp.dot(a_ref[...], w_ref[...], preferred_element_type=F32)


def matmul_residual(a, w, h):
    rows, k = a.shape
    n = w.shape[1]
    return pl.pallas_call(
        _matmul_residual_kernel,
        grid=(rows // ROW_TILE,),
        in_specs=[
            pl.BlockSpec((ROW_TILE, k), lambda i: (i, 0)),
            pl.BlockSpec((k, n), lambda i: (0, 0)),
            pl.BlockSpec((ROW_TILE, n), lambda i: (i, 0)),
        ],
        out_specs=pl.BlockSpec((ROW_TILE, n), lambda i: (i, 0)),
        out_shape=jax.ShapeDtypeStruct((rows, n), F32),
        compiler_params=_params("arbitrary"),
        name="matmul_residual",
    )(a, w, h)


def _gla_kernel(q_ref, f_ref, i_ref, g_ref, lb_ref, gain_ref, o_ref,
                st_ref, b_scr, k_scr, *, layer, n_chunks):
    C, SUB = GLA_CHUNK, GLA_SUB
    n_sub = C // SUB

    @pl.when(pl.program_id(2) == 0)
    def _():
        st_ref[...] = jnp.zeros_like(st_ref)

    lbp = lb_ref[...]
    e = jnp.exp(lbp - jnp.max(lbp, axis=0, keepdims=True))
    p = e / jnp.sum(e, axis=0, keepdims=True)
    lb = jnp.sum(p[:layer + 1], axis=0, keepdims=True) - p[0:1]
    one_m_lb = 1.0 - lb
    gain = gain_ref[...]

    row = lax.broadcasted_iota(jnp.int32, (C, C), 0)
    col = lax.broadcasted_iota(jnp.int32, (C, C), 1)
    tril = (row >= col).astype(BF16)
    sub_row = lax.broadcasted_iota(jnp.int32, (SUB, 1), 0)
    sub_col = lax.broadcasted_iota(jnp.int32, (SUB, C), 1)

    def chunk(c, carry):
        r0 = pl.multiple_of(c * C, C)
        q = q_ref[pl.ds(r0, C), :]
        q = q * _sigmoid(q)
        sig = _sigmoid(f_ref[pl.ds(r0, C), :])
        f = lb + one_m_lb * sig
        lf = jnp.log(jnp.maximum(f, F_FLOOR))
        k = one_m_lb * (1.0 - sig)
        v = i_ref[pl.ds(r0, C), :]
        v16 = v.astype(BF16)

        lf_hi = lf.astype(BF16)
        rem = lf - lf_hi.astype(F32)
        lf_mid = rem.astype(BF16)
        lf_lo = (rem - lf_mid.astype(F32)).astype(BF16)
        b = (jnp.dot(tril, lf_hi, preferred_element_type=F32)
             + jnp.dot(tril, lf_mid, preferred_element_type=F32)
             + jnp.dot(tril, lf_lo, preferred_element_type=F32))
        b_scr[...] = b
        k_scr[...] = k
        b_last = b_scr[C - 1:C, :]

        st = st_ref[...]
        qb = (q * jnp.exp(b)).astype(BF16)
        o_inter = lax.dot_general(qb, st.astype(BF16), (((1,), (1,)), ((), ())),
                                  preferred_element_type=F32)

        a_rows = [jnp.zeros((SUB, C), F32)]
        o_diag = []
        for i in range(n_sub):
            lo = i * SUB
            q_i = q[lo:lo + SUB]
            b_i = b[lo:lo + SUB]
            if i > 0:
                r_i = b_scr[lo - 1:lo, :]
                q_s = (q_i * jnp.exp(b_i - r_i)).astype(BF16)
                k_s = (k * jnp.exp(jnp.minimum(r_i - b, 0.0))).astype(BF16)
                a_i = lax.dot_general(q_s, k_s, (((1,), (1,)), ((), ())),
                                      preferred_element_type=F32)
                a_rows.append(jnp.where(sub_col < lo, a_i, 0.0))
            acc = jnp.zeros((SUB, HG_DK), F32)
            for s in range(SUB):
                b_s = b_scr[lo + s:lo + s + 1, :]
                k_row = k_scr[lo + s:lo + s + 1, :]
                v_row = i_ref[pl.ds(r0 + lo + s, 1), :]
                w = jnp.sum(q_i * k_row * jnp.exp(b_i - b_s), axis=-1, keepdims=True)
                acc = acc + jnp.where(sub_row >= s, w, 0.0) * v_row
            o_diag.append(acc)

        a_off = jnp.concatenate(a_rows, axis=0).astype(BF16)
        o = (o_inter + jnp.dot(a_off, v16, preferred_element_type=F32)
             + jnp.concatenate(o_diag, axis=0))

        kd = (k * jnp.exp(b_last - b)).astype(BF16)
        st_ref[...] = st * jnp.exp(b_last) + lax.dot_general(
            v16, kd, (((0,), (0,)), ((), ())), preferred_element_type=F32)

        g = g_ref[pl.ds(r0, C), :]
        o = o * lax.rsqrt(jnp.mean(o * o, axis=-1, keepdims=True) + EPS)
        o_ref[pl.ds(r0, C), :] = (o * gain * (g * _sigmoid(g))).astype(o_ref.dtype)
        return carry

    lax.fori_loop(0, n_chunks, chunk, 0, unroll=2)


def gla_mixer(proj, hg_lb, gain, layer, batch, l_pad):
    rows = proj.shape[0]
    n_t = l_pad // ROW_TILE
    n_layers = hg_lb.shape[0]
    col = lambda off: (lambda b, h, t: (b * n_t + t, off + h))
    return pl.pallas_call(
        functools.partial(_gla_kernel, layer=layer, n_chunks=ROW_TILE // GLA_CHUNK),
        grid=(batch, HG_HEADS, n_t),
        in_specs=[
            pl.BlockSpec((ROW_TILE, HG_DK), col(0)),
            pl.BlockSpec((ROW_TILE, HG_DK), col(HG_HEADS)),
            pl.BlockSpec((ROW_TILE, HG_DK), col(2 * HG_HEADS)),
            pl.BlockSpec((ROW_TILE, HG_DK), col(3 * HG_HEADS)),
            pl.BlockSpec((n_layers, HG_DK), lambda b, h, t: (0, h)),
            pl.BlockSpec((1, HG_DK), lambda b, h, t: (0, h)),
        ],
        out_specs=pl.BlockSpec((ROW_TILE, HG_DK), col(0)),
        out_shape=jax.ShapeDtypeStruct((rows, D_MODEL), BF16),
        scratch_shapes=[
            pltpu.VMEM((HG_DK, HG_DK), F32),
            pltpu.VMEM((GLA_CHUNK, HG_DK), F32),
            pltpu.VMEM((GLA_CHUNK, HG_DK), F32),
        ],
        compiler_params=_params("arbitrary", "arbitrary", "arbitrary"),
        name="gla_mixer",
    )(proj, proj, proj, proj, hg_lb, gain.reshape(1, D_MODEL))


def _ffn_up_kernel(x_ref, g_ref, wa_ref, wb_ref, cwa_ref, cwb_ref, cba_ref, cbb_ref,
                   o_ref, xn_ref, carry_ref, *, tiles_per_seq):
    i = pl.program_id(0)
    j = pl.program_id(1)

    @pl.when(j == 0)
    def _():
        xn_ref[...] = _rms_rows(x_ref[...], g_ref[...]).astype(BF16)

    xn = xn_ref[...]
    tm = xn.shape[0]
    row = lax.broadcasted_iota(jnp.int32, (tm, 1), 0)

    @pl.when((i % tiles_per_seq) == 0)
    def _():
        carry_ref[j] = jnp.zeros(carry_ref.shape[1:], F32)

    def conv(u, cw_ref, cb_ref, slot):
        tail = carry_ref[j, slot]
        carry_ref[j, slot] = u[tm - 8:, :]
        u1 = jnp.where(row >= 1, pltpu.roll(u, 1, axis=0), tail[7:8, :])
        u2 = jnp.where(row >= 2, pltpu.roll(u, 2, axis=0),
                       jnp.where(row == 0, tail[6:7, :], tail[7:8, :]))
        return (u2 * cw_ref[0:1, :] + u1 * cw_ref[1:2, :] + u * cw_ref[2:3, :]
                + cb_ref[...])

    ua = jnp.dot(xn, wa_ref[...], preferred_element_type=F32)
    ub = jnp.dot(xn, wb_ref[...], preferred_element_type=F32)
    a = conv(ua, cwa_ref, cba_ref, 0)
    b = conv(ub, cwb_ref, cbb_ref, 1)
    o_ref[...] = (a * _sigmoid(a) * b).astype(o_ref.dtype)


def ffn_up(h, g, w_up, conv_w, conv_b, l_pad):
    rows, d = h.shape
    tn = FFN_COL_TILE
    nj = D_FF // tn
    return pl.pallas_call(
        functools.partial(_ffn_up_kernel, tiles_per_seq=l_pad // ROW_TILE),
        grid=(rows // ROW_TILE, nj),
        in_specs=[
            pl.BlockSpec((ROW_TILE, d), lambda i, j: (i, 0)),
            pl.BlockSpec((1, d), lambda i, j: (0, 0)),
            pl.BlockSpec((d, tn), lambda i, j: (0, j)),
            pl.BlockSpec((d, tn), lambda i, j: (0, j + nj)),
            pl.BlockSpec((CONV_W, tn), lambda i, j: (0, j)),
            pl.BlockSpec((CONV_W, tn), lambda i, j: (0, j + nj)),
            pl.BlockSpec((1, tn), lambda i, j: (0, j)),
            pl.BlockSpec((1, tn), lambda i, j: (0, j + nj)),
        ],
        out_specs=pl.BlockSpec((ROW_TILE, tn), lambda i, j: (i, j)),
        out_shape=jax.ShapeDtypeStruct((rows, D_FF), BF16),
        scratch_shapes=[
            pltpu.VMEM((ROW_TILE, d), BF16),
            pltpu.VMEM((nj, 2, 8, tn), F32),
        ],
        compiler_params=_params("arbitrary", "arbitrary"),
        name="ffn_up",
    )(h, g.reshape(1, d), w_up, w_up, conv_w, conv_w,
      conv_b.reshape(1, 2 * D_FF), conv_b.reshape(1, 2 * D_FF))


def _diff_attn_kernel(i_tab, j_tab, q_ref, k_ref, vt_ref, lq1_ref, lk1_ref, lq2_ref, lk2_ref,
                      gain_ref, o_ref, qs_ref, m_ref, l_ref, acc_ref, *, lam_init):
    p_id = pl.program_id(2)
    i = i_tab[p_id]
    j = j_tab[p_id]
    tq = q_ref.shape[0]
    tk = k_ref.shape[0]
    sub = ATT_SUB
    per_map = tq // sub
    n_sub = 2 * per_map

    @pl.when(j == 0)
    def _():
        lane = lax.broadcasted_iota(jnp.int32, (sub, 2 * DA_DH), 1)
        for c in range(per_map):
            q = q_ref[c * sub:(c + 1) * sub, :]
            qs_ref[c] = jnp.where(lane < DA_DH, q, jnp.zeros_like(q))
            qs_ref[per_map + c] = jnp.where(lane >= DA_DH, q, jnp.zeros_like(q))
        m_ref[...] = jnp.full_like(m_ref, NEG_BIG)
        l_ref[...] = jnp.zeros_like(l_ref)
        acc_ref[...] = jnp.zeros_like(acc_ref)

    def step(masked):
        def n_keys(c):
            return ((c % per_map) + 1) * sub if masked else tk

        def scores(c):
            return lax.dot_general(k_ref[0:n_keys(c), :], qs_ref[c], (((1,), (1,)), ((), ())),
                                   preferred_element_type=F32)

        def softmax(c, s):
            if masked:
                key = lax.broadcasted_iota(jnp.int32, s.shape, 0)
                qry = lax.broadcasted_iota(jnp.int32, s.shape, 1) + (c % per_map) * sub
                s = jnp.where(key <= qry, s, NEG_BIG)
            m_prev = m_ref[c]
            m_new = jnp.maximum(m_prev, jnp.max(s, axis=0, keepdims=True))
            alpha = jnp.exp2(m_prev - m_new)
            p = jnp.exp2(s - m_new)
            l_ref[c] = alpha * l_ref[c] + jnp.sum(p, axis=0, keepdims=True)
            m_ref[c] = m_new
            return alpha, p.astype(BF16)

        s_q = [scores(0), scores(1)]
        ap = softmax(0, s_q.pop(0))
        for c in range(n_sub):
            if c + 2 < n_sub:
                s_q.append(scores(c + 2))
            alpha, p = ap
            if c + 1 < n_sub:
                ap = softmax(c + 1, s_q.pop(0))
            acc_ref[c] = alpha * acc_ref[c] + jnp.dot(
                vt_ref[:, 0:n_keys(c)], p, preferred_element_type=F32)

    @pl.when(j < i)
    def _():
        step(False)

    @pl.when(j == i)
    def _():
        step(True)
        lam = (jnp.exp(jnp.sum(lq1_ref[...] * lk1_ref[...], axis=-1, keepdims=True))
               - jnp.exp(jnp.sum(lq2_ref[...] * lk2_ref[...], axis=-1, keepdims=True))
               + lam_init)
        for c in range(per_map):
            o_t = (acc_ref[c] / l_ref[c]
                   - lam * (acc_ref[per_map + c] / l_ref[per_map + c]))
            o = o_t.T
            o = o * lax.rsqrt(jnp.mean(o * o, axis=-1, keepdims=True) + EPS) * gain_ref[...]
            o_ref[c * sub:(c + 1) * sub, :] = (o * (1.0 - lam_init)).astype(o_ref.dtype)


def diff_attention_core(q, k, vt, lq1, lk1, lq2, lk2, sub_gain, lam_init, batch, l_pad):
    rows = q.shape[0]
    t = ROW_TILE
    n_t = l_pad // t
    n_sub = 2 * t // ATT_SUB
    pairs = [(i, j) for i in range(n_t) for j in range(i + 1)]
    i_tab = jnp.array([p[0] for p in pairs], jnp.int32)
    j_tab = jnp.array([p[1] for p in pairs], jnp.int32)
    vec = lambda a: a.reshape(1, -1)
    small = lambda n: pl.BlockSpec((1, n), lambda b, h, p, it, jt: (0, 0))
    grid_spec = pltpu.PrefetchScalarGridSpec(
        num_scalar_prefetch=2,
        grid=(batch, DA_HEADS, len(pairs)),
        in_specs=[
            pl.BlockSpec((t, 2 * DA_DH), lambda b, h, p, it, jt: (b * n_t + it[p], h)),
            pl.BlockSpec((t, 2 * DA_DH), lambda b, h, p, it, jt: (b * n_t + jt[p], h)),
            pl.BlockSpec((DA_DV, t), lambda b, h, p, it, jt: (h, b * n_t + jt[p])),
            small(DA_DH), small(DA_DH), small(DA_DH), small(DA_DH), small(DA_DV),
        ],
        out_specs=pl.BlockSpec((t, DA_DV), lambda b, h, p, it, jt: (b * n_t + it[p], h)),
        scratch_shapes=[
            pltpu.VMEM((n_sub, ATT_SUB, 2 * DA_DH), BF16),
            pltpu.VMEM((n_sub, 1, ATT_SUB), F32),
            pltpu.VMEM((n_sub, 1, ATT_SUB), F32),
            pltpu.VMEM((n_sub, DA_DV, ATT_SUB), F32),
        ],
    )
    return pl.pallas_call(
        functools.partial(_diff_attn_kernel, lam_init=lam_init),
        grid_spec=grid_spec,
        out_shape=jax.ShapeDtypeStruct((rows, D_MODEL), BF16),
        compiler_params=_params("arbitrary", "arbitrary", "arbitrary"),
        name="diff_attention",
    )(i_tab, j_tab, q, k, vt, vec(lq1), vec(lk1), vec(lq2), vec(lk2), vec(sub_gain))


def _final_norm_kernel(x_ref, g_ref, o_ref):
    o_ref[...] = _rms_rows(x_ref[...], g_ref[...])


def final_norm(h, g):
    rows, d = h.shape
    return pl.pallas_call(
        _final_norm_kernel,
        grid=(rows // ROW_TILE,),
        in_specs=[pl.BlockSpec((ROW_TILE, d), lambda i: (i, 0)),
                  pl.BlockSpec((1, d), lambda i: (0, 0))],
        out_specs=pl.BlockSpec((ROW_TILE, d), lambda i: (i, 0)),
        out_shape=jax.ShapeDtypeStruct((rows, d), F32),
        compiler_params=_params("arbitrary"),
        name="final_norm",
    )(h, g.reshape(1, d))


def kernel(x, meta_tokens, norm_mix, norm_ffn, hg_w_in, hg_lb, hg_gain_o, hg_w_o, kv_norm, w_kv, da_w_q, da_lam_q1, da_lam_k1, da_lam_q2, da_lam_k2, da_sub_gain, da_w_o, ffn_w_up, ffn_conv_w, ffn_conv_b, ffn_w_down, final_norm_g):
    batch, seq, d = x.shape
    n_a = hg_w_in.shape[0]
    depth = norm_mix.shape[0]
    l_real = N_META + seq
    l_pad = -(-l_real // L_ALIGN) * L_ALIGN

    meta = jnp.broadcast_to(meta_tokens[None].astype(x.dtype), (batch, N_META, d))
    tail = jnp.zeros((batch, l_pad - l_real, d), x.dtype)
    h = jnp.concatenate([meta, x, tail], axis=1).reshape(batch * l_pad, d)

    k_sh = vt_sh = None
    for l in range(depth):
        if l < n_a:
            proj = norm_matmul(h, norm_mix[l], hg_w_in[l].astype(BF16), F32)
            o = gla_mixer(proj, hg_lb, hg_gain_o[l], l, batch, l_pad)
            h = matmul_residual(o, hg_w_o[l].astype(BF16), h)
        else:
            j = l - n_a
            lam_init = 0.8 - 0.6 * math.exp(-0.3 * l)
            q = norm_matmul(h, norm_mix[l], da_w_q[j].astype(BF16), BF16,
                            scale=DA_DH ** -0.5 * LOG2E)
            o = diff_attention_core(q, k_sh, vt_sh, da_lam_q1[j], da_lam_k1[j], da_lam_q2[j],
                                    da_lam_k2[j], da_sub_gain[j], lam_init, batch, l_pad)
            h = matmul_residual(o, da_w_o[j].astype(BF16), h)
        gated = ffn_up(h, norm_ffn[l], ffn_w_up[l].astype(BF16), ffn_conv_w[l],
                       ffn_conv_b[l], l_pad)
        h = matmul_residual(gated, ffn_w_down[l].astype(BF16), h)
        if l == n_a - 1:
            k_sh = norm_matmul(h, kv_norm, w_kv[:, :DA_QK].astype(BF16), BF16)
            vt_sh = norm_matmul_t(h, kv_norm, w_kv[:, DA_QK:].T.astype(BF16), BF16)

    out = final_norm(h, final_norm_g).reshape(batch, l_pad, d)
    return out[:, N_META:l_real]
```

```python
import functools
import math

import jax
import jax.numpy as jnp
from jax import lax
from jax.experimental import pallas as pl
from jax.experimental.pallas import tpu as pltpu

D_MODEL = 1024
N_META = 16
HG_HEADS = 8
HG_DK = 128
HG_FDIM = HG_HEADS * HG_DK
F_FLOOR = 1e-30
DA_HEADS = 8
DA_DH = 64
DA_DV = 128
DA_QK = DA_HEADS * 2 * DA_DH
D_FF = 2816
CONV_W = 3
EPS = 1e-5

LANES = 128
MXU_DIM = 256
ROW_TILE = 3 * MXU_DIM
L_ALIGN = ROW_TILE
ATT_SUB = MXU_DIM
LOG2E = 1.4426950408889634
GLA_CHUNK = 64
GLA_SUB = 16
FFN_COL_TILE = 256
PROJ_COL_TILE = 1024
VMEM_LIMIT = 56 * 1024 * 1024
NEG_BIG = -1e30
ATT_NOSHIFT_LIMIT = 48.0
GLA_HEADS_PER_STEP = 2
FFN_ROW_SUB = MXU_DIM
PROJ_ROW_SUB = MXU_DIM

BF16 = jnp.bfloat16
F32 = jnp.float32


def _params(*sem):
    return pltpu.CompilerParams(dimension_semantics=sem, vmem_limit_bytes=VMEM_LIMIT)


def _sigmoid(x):
    return 1.0 / (1.0 + jnp.exp(-x))


def _rms_rows(x, g):
    return x * lax.rsqrt(jnp.mean(x * x, axis=-1, keepdims=True) + EPS) * g


def _norm_matmul_kernel(x_ref, g_ref, w_ref, o_ref, *, scale):
    tm, n = o_ref.shape
    rs = PROJ_ROW_SUB
    tn = min(n, PROJ_COL_TILE)

    def normed(r):
        return _rms_rows(x_ref[r * rs:(r + 1) * rs, :], g_ref[...]).astype(BF16)

    xn_next = normed(0)
    for r in range(tm // rs):
        xn = xn_next
        if (r + 1) * rs < tm:
            xn_next = normed(r + 1)
        for c in range(n // tn):
            y = jnp.dot(xn, w_ref[:, c * tn:(c + 1) * tn], preferred_element_type=F32)
            if scale != 1.0:
                y = y * scale
            o_ref[r * rs:(r + 1) * rs, c * tn:(c + 1) * tn] = y.astype(o_ref.dtype)


def norm_matmul(x, g, w, out_dtype, scale=1.0):
    rows, d = x.shape
    n = w.shape[1]
    return pl.pallas_call(
        functools.partial(_norm_matmul_kernel, scale=scale),
        grid=(rows // ROW_TILE,),
        in_specs=[
            pl.BlockSpec((ROW_TILE, d), lambda i: (i, 0)),
            pl.BlockSpec((1, d), lambda i: (0, 0)),
            pl.BlockSpec((d, n), lambda i: (0, 0), pipeline_mode=pl.Buffered(1)),
        ],
        out_specs=pl.BlockSpec((ROW_TILE, n), lambda i: (i, 0)),
        out_shape=jax.ShapeDtypeStruct((rows, n), out_dtype),
        compiler_params=_params("arbitrary"),
        name="norm_matmul",
    )(x, g.reshape(1, d), w)


def _norm_matmul_t_kernel(x_ref, g_ref, wt_ref, o_ref):
    xn = _rms_rows(x_ref[...], g_ref[...]).astype(BF16)
    o_ref[...] = lax.dot_general(wt_ref[...], xn, (((1,), (1,)), ((), ())),
                                 preferred_element_type=F32).astype(o_ref.dtype)


def norm_matmul_t(x, g, wt, out_dtype):
    rows, d = x.shape
    n = wt.shape[0]
    return pl.pallas_call(
        _norm_matmul_t_kernel,
        grid=(rows // ROW_TILE,),
        in_specs=[
            pl.BlockSpec((ROW_TILE, d), lambda i: (i, 0)),
            pl.BlockSpec((1, d), lambda i: (0, 0)),
            pl.BlockSpec((n, d), lambda i: (0, 0)),
        ],
        out_specs=pl.BlockSpec((None, n, ROW_TILE), lambda i: (i, 0, 0)),
        out_shape=jax.ShapeDtypeStruct((rows // ROW_TILE, n, ROW_TILE), out_dtype),
        compiler_params=_params("arbitrary"),
        name="norm_matmul_t",
    )(x, g.reshape(1, d), wt)


def _matmul_residual_kernel(a_ref, w_ref, h_ref, o_ref):
    o_ref[...] = h_ref[...] + jnp.dot(a_ref[...], w_ref[...], preferred_element_type=F32)


def _matmul_residual_norm_kernel(a_ref, w_ref, h_ref, g_ref, o_ref):
    y = h_ref[...] + jnp.dot(a_ref[...], w_ref[...], preferred_element_type=F32)
    o_ref[...] = _rms_rows(y, g_ref[...])


def matmul_residual(a, w, h, norm_g=None):
    rows, k = a.shape
    n = w.shape[1]
    in_specs = [
        pl.BlockSpec((ROW_TILE, k), lambda i: (i, 0)),
        pl.BlockSpec((k, n), lambda i: (0, 0), pipeline_mode=pl.Buffered(1)),
        pl.BlockSpec((ROW_TILE, n), lambda i: (i, 0)),
    ]
    args = (a, w, h)
    body = _matmul_residual_kernel
    if norm_g is not None:
        in_specs.append(pl.BlockSpec((1, n), lambda i: (0, 0)))
        args += (norm_g.reshape(1, n),)
        body = _matmul_residual_norm_kernel
    return pl.pallas_call(
        body,
        grid=(rows // ROW_TILE,),
        in_specs=in_specs,
        out_specs=pl.BlockSpec((ROW_TILE, n), lambda i: (i, 0)),
        out_shape=jax.ShapeDtypeStruct((rows, n), F32),
        compiler_params=_params("arbitrary"),
        name="matmul_residual",
    )(*args)


def _gla_kernel(q_ref, f_ref, i_ref, g_ref, lb_ref, gain_ref, o_ref,
                st_ref, b_scr, k_scr, v_scr, *, layer, n_chunks):
    C, SUB, HALF = GLA_CHUNK, GLA_SUB, GLA_SUB // 2
    n_sub = C // SUB
    heads = range(GLA_HEADS_PER_STEP)
    lanes = [slice(h * HG_DK, (h + 1) * HG_DK) for h in heads]

    @pl.when(pl.program_id(2) == 0)
    def _():
        st_ref[...] = jnp.zeros_like(st_ref)

    lbp = lb_ref[...]
    e = jnp.exp(lbp - jnp.max(lbp, axis=0, keepdims=True))
    p = e / jnp.sum(e, axis=0, keepdims=True)
    lb_all = jnp.sum(p[:layer + 1], axis=0, keepdims=True) - p[0:1]
    gain_all = gain_ref[...]

    row = lax.broadcasted_iota(jnp.int32, (C, C), 0)
    col = lax.broadcasted_iota(jnp.int32, (C, C), 1)
    tril = (row >= col).astype(BF16)
    half_row = lax.broadcasted_iota(jnp.int32, (HALF, 1), 0)
    sub_col = lax.broadcasted_iota(jnp.int32, (SUB, C), 1)
    nt = (((1,), (1,)), ((), ()))

    def chunk(c, carry):
        r0 = pl.multiple_of(c * C, C)
        rows = pl.ds(r0, C)
        q, k, v16, b = [], [], [], []
        for h in heads:
            lb = lb_all[:, lanes[h]]
            qh = q_ref[rows, lanes[h]]
            q.append(qh * _sigmoid(qh))
            sig = _sigmoid(f_ref[rows, lanes[h]])
            lf = jnp.log(jnp.maximum(lb + (1.0 - lb) * sig, F_FLOOR))
            k.append((1.0 - lb) * (1.0 - sig))
            v = i_ref[rows, lanes[h]]
            v_scr[h] = v
            v16.append(v.astype(BF16))
            lf_hi = lf.astype(BF16)
            rem = lf - lf_hi.astype(F32)
            lf_mid = rem.astype(BF16)
            lf_lo = (rem - lf_mid.astype(F32)).astype(BF16)
            b.append(jnp.dot(tril, lf_hi, preferred_element_type=F32)
                     + jnp.dot(tril, lf_mid, preferred_element_type=F32)
                     + jnp.dot(tril, lf_lo, preferred_element_type=F32))

        st, o_inter, a_off, kv, decay = [], [], [], [], []
        for h in heads:
            b_scr[h] = b[h]
            k_scr[h] = k[h]
            b_last = b_scr[h, C - 1:C, :]
            st.append(st_ref[h])
            qb = (q[h] * jnp.exp(b[h])).astype(BF16)
            o_inter.append(lax.dot_general(qb, st[h].astype(BF16), nt,
                                           preferred_element_type=F32))
            kd = (k[h] * jnp.exp(b_last - b[h])).astype(BF16)
            kv.append(lax.dot_general(v16[h], kd, (((0,), (0,)), ((), ())),
                                      preferred_element_type=F32))
            decay.append(jnp.exp(b_last))
            a_rows = [jnp.zeros((SUB, C), F32)]
            for i in range(1, n_sub):
                lo = i * SUB
                r_i = b_scr[h, lo - 1:lo, :]
                q_s = (q[h][lo:lo + SUB] * jnp.exp(b[h][lo:lo + SUB] - r_i)).astype(BF16)
                k_s = (k[h] * jnp.exp(jnp.minimum(r_i - b[h], 0.0))).astype(BF16)
                a_i = lax.dot_general(q_s, k_s, nt, preferred_element_type=F32)
                a_rows.append(jnp.where(sub_col < lo, a_i, 0.0))
            a_off.append(jnp.concatenate(a_rows, axis=0).astype(BF16))

        o_diag = []
        for h in heads:
            pieces = []
            for i in range(n_sub):
                lo = i * SUB
                q_lo, q_hi = q[h][lo:lo + HALF], q[h][lo + HALF:lo + SUB]
                b_lo, b_hi = b[h][lo:lo + HALF], b[h][lo + HALF:lo + SUB]
                acc_lo = jnp.zeros((HALF, HG_DK), F32)
                acc_hi = jnp.zeros((HALF, HG_DK), F32)
                for s in range(SUB):
                    b_s = b_scr[h, lo + s:lo + s + 1, :]
                    k_row = k_scr[h, lo + s:lo + s + 1, :]
                    v_row = v_scr[h, lo + s:lo + s + 1, :]
                    w_hi = jnp.sum(q_hi * k_row * jnp.exp(b_hi - b_s), axis=-1, keepdims=True)
                    if s < HALF:
                        w_lo = jnp.sum(q_lo * k_row * jnp.exp(b_lo - b_s), axis=-1,
                                       keepdims=True)
                        acc_lo = acc_lo + jnp.where(half_row >= s, w_lo, 0.0) * v_row
                        acc_hi = acc_hi + w_hi * v_row
                    else:
                        acc_hi = acc_hi + jnp.where(half_row >= s - HALF, w_hi, 0.0) * v_row
                pieces += [acc_lo, acc_hi]
            o_diag.append(jnp.concatenate(pieces, axis=0))

        for h in heads:
            o = (o_inter[h] + jnp.dot(a_off[h], v16[h], preferred_element_type=F32)
                 + o_diag[h])
            st_ref[h] = st[h] * decay[h] + kv[h]
            g = g_ref[rows, lanes[h]]
            o = o * lax.rsqrt(jnp.mean(o * o, axis=-1, keepdims=True) + EPS)
            o_ref[rows, lanes[h]] = (o * gain_all[:, lanes[h]] * (g * _sigmoid(g))
                                     ).astype(o_ref.dtype)
        return carry

    lax.fori_loop(0, n_chunks, chunk, 0, unroll=2)


def gla_mixer(proj, hg_lb, gain, layer, batch, l_pad):
    rows = proj.shape[0]
    n_t = l_pad // ROW_TILE
    n_layers = hg_lb.shape[0]
    hps = GLA_HEADS_PER_STEP
    width = hps * HG_DK
    groups = HG_HEADS // hps
    col = lambda part: (lambda b, h, t: (b * n_t + t, part * groups + h))
    return pl.pallas_call(
        functools.partial(_gla_kernel, layer=layer, n_chunks=ROW_TILE // GLA_CHUNK),
        grid=(batch, groups, n_t),
        in_specs=[
            pl.BlockSpec((ROW_TILE, width), col(0)),
            pl.BlockSpec((ROW_TILE, width), col(1)),
            pl.BlockSpec((ROW_TILE, width), col(2)),
            pl.BlockSpec((ROW_TILE, width), col(3)),
            pl.BlockSpec((n_layers, width), lambda b, h, t: (0, h)),
            pl.BlockSpec((1, width), lambda b, h, t: (0, h)),
        ],
        out_specs=pl.BlockSpec((ROW_TILE, width), col(0)),
        out_shape=jax.ShapeDtypeStruct((rows, D_MODEL), BF16),
        scratch_shapes=[
            pltpu.VMEM((hps, HG_DK, HG_DK), F32),
            pltpu.VMEM((hps, GLA_CHUNK, HG_DK), F32),
            pltpu.VMEM((hps, GLA_CHUNK, HG_DK), F32),
            pltpu.VMEM((hps, GLA_CHUNK, HG_DK), F32),
        ],
        compiler_params=_params("arbitrary", "arbitrary", "arbitrary"),
        name="gla_mixer",
    )(proj, proj, proj, proj, hg_lb, gain.reshape(1, D_MODEL))


def _ffn_up_kernel(x_ref, g_ref, wa_ref, wb_ref, cwa_ref, cwb_ref, cba_ref, cbb_ref,
                   o_ref, xn_ref, u_ref, carry_ref, *, tiles_per_seq):
    i = pl.program_id(0)
    j = pl.program_id(1)
    tm = xn_ref.shape[0]
    rs = FFN_ROW_SUB
    halo = 8

    @pl.when(j == 0)
    def _():
        xn_ref[...] = _rms_rows(x_ref[...], g_ref[...]).astype(BF16)

    @pl.when((i % tiles_per_seq) == 0)
    def _():
        carry_ref[j] = jnp.zeros(carry_ref.shape[1:], F32)

    u_ref[:, 0:halo, :] = carry_ref[j]

    def project(r):
        xr = xn_ref[r * rs:(r + 1) * rs, :]
        return (jnp.dot(xr, wa_ref[...], preferred_element_type=F32),
                jnp.dot(xr, wb_ref[...], preferred_element_type=F32))

    def conv(slot, r, cw_ref, cb_ref):
        base = r * rs
        ue = u_ref[slot, base:base + halo + rs, :]
        u1 = pltpu.roll(ue, 1, axis=0)[halo:]
        u2 = pltpu.roll(ue, 2, axis=0)[halo:]
        return (u2 * cw_ref[0:1, :] + u1 * cw_ref[1:2, :] + ue[halo:] * cw_ref[2:3, :]
                + cb_ref[...])

    u_next = project(0)
    for r in range(tm // rs):
        ua, ub = u_next
        u_ref[0, halo + r * rs:halo + (r + 1) * rs, :] = ua
        u_ref[1, halo + r * rs:halo + (r + 1) * rs, :] = ub
        if (r + 1) * rs < tm:
            u_next = project(r + 1)
        a = conv(0, r, cwa_ref, cba_ref)
        b = conv(1, r, cwb_ref, cbb_ref)
        o_ref[r * rs:(r + 1) * rs, :] = (a * _sigmoid(a) * b).astype(o_ref.dtype)
    carry_ref[j] = u_ref[:, tm:tm + halo, :]


def ffn_up(h, g, w_up, conv_w, conv_b, l_pad):
    rows, d = h.shape
    tn = FFN_COL_TILE
    nj = D_FF // tn
    return pl.pallas_call(
        functools.partial(_ffn_up_kernel, tiles_per_seq=l_pad // ROW_TILE),
        grid=(rows // ROW_TILE, nj),
        in_specs=[
            pl.BlockSpec((ROW_TILE, d), lambda i, j: (i, 0)),
            pl.BlockSpec((1, d), lambda i, j: (0, 0)),
            pl.BlockSpec((d, tn), lambda i, j: (0, j)),
            pl.BlockSpec((d, tn), lambda i, j: (0, j + nj)),
            pl.BlockSpec((CONV_W, tn), lambda i, j: (0, j)),
            pl.BlockSpec((CONV_W, tn), lambda i, j: (0, j + nj)),
            pl.BlockSpec((1, tn), lambda i, j: (0, j)),
            pl.BlockSpec((1, tn), lambda i, j: (0, j + nj)),
        ],
        out_specs=pl.BlockSpec((ROW_TILE, tn), lambda i, j: (i, j)),
        out_shape=jax.ShapeDtypeStruct((rows, D_FF), BF16),
        scratch_shapes=[
            pltpu.VMEM((ROW_TILE, d), BF16),
            pltpu.VMEM((2, ROW_TILE + 8, tn), F32),
            pltpu.VMEM((nj, 2, 8, tn), F32),
        ],
        compiler_params=_params("arbitrary", "arbitrary"),
        name="ffn_up",
    )(h, g.reshape(1, d), w_up, w_up, conv_w, conv_w,
      conv_b.reshape(1, 2 * D_FF), conv_b.reshape(1, 2 * D_FF))


def _diff_attn_kernel(q_ref, k_ref, vt_ref, lq1_ref, lk1_ref, lq2_ref, lk2_ref, gain_ref,
                      o_ref, qs_ref, m_ref, l_ref, acc_ref, kmax_ref, *, lam_init):
    i = pl.program_id(2)
    tq = q_ref.shape[0]
    tk = tq
    n_t = vt_ref.shape[0]
    sub = ATT_SUB
    per_map = tq // sub
    n_sub = 2 * per_map

    @pl.when(i == 0)
    def _():
        def body(t, mx):
            kb = k_ref[pl.ds(pl.multiple_of(t * tk, tk), tk), :].astype(F32)
            return jnp.maximum(mx, jnp.max(jnp.sum(kb * kb, axis=-1, keepdims=True)))
        kmax_ref[0] = lax.fori_loop(0, n_t, body, jnp.float32(0.0))

    q = q_ref[...]
    lane = lax.broadcasted_iota(jnp.int32, (sub, 2 * DA_DH), 1)
    for c in range(per_map):
        q_c = q[c * sub:(c + 1) * sub, :]
        qs_ref[c] = jnp.where(lane < DA_DH, q_c, jnp.zeros_like(q_c))
        qs_ref[per_map + c] = jnp.where(lane >= DA_DH, q_c, jnp.zeros_like(q_c))
    l_ref[...] = jnp.zeros_like(l_ref)
    acc_ref[...] = jnp.zeros_like(acc_ref)
    qf = q.astype(F32)
    qmax = jnp.max(jnp.sum(qf * qf, axis=-1, keepdims=True))
    no_shift = qmax * kmax_ref[0] <= ATT_NOSHIFT_LIMIT ** 2

    def kv_block(j, masked, shifted):
        row0 = pl.multiple_of(j * tk, tk)

        def n_keys(c):
            return ((c % per_map) + 1) * sub if masked else tk

        def scores(c):
            return lax.dot_general(k_ref[pl.ds(row0, n_keys(c)), :], qs_ref[c],
                                   (((1,), (1,)), ((), ())),
                                   preferred_element_type=F32)

        def softmax(c, s):
            if masked:
                key = lax.broadcasted_iota(jnp.int32, s.shape, 0)
                qry = lax.broadcasted_iota(jnp.int32, s.shape, 1) + (c % per_map) * sub
                s = jnp.where(key <= qry, s, NEG_BIG)
            if not shifted:
                p = jnp.exp2(s)
                l_ref[c] = l_ref[c] + jnp.sum(p, axis=0, keepdims=True)
                return None, p.astype(BF16)
            m_prev = m_ref[c]
            m_new = jnp.maximum(m_prev, jnp.max(s, axis=0, keepdims=True))
            alpha = jnp.exp2(m_prev - m_new)
            p = jnp.exp2(s - m_new)
            l_ref[c] = alpha * l_ref[c] + jnp.sum(p, axis=0, keepdims=True)
            m_ref[c] = m_new
            return alpha, p.astype(BF16)

        s_q = [scores(0), scores(1)]
        ap = softmax(0, s_q.pop(0))
        for c in range(n_sub):
            if c + 2 < n_sub:
                s_q.append(scores(c + 2))
            alpha, p = ap
            if c + 1 < n_sub:
                ap = softmax(c + 1, s_q.pop(0))
            pv = jnp.dot(vt_ref[j, :, pl.ds(0, n_keys(c))], p,
                         preferred_element_type=F32)
            acc_ref[c] = (acc_ref[c] if alpha is None else alpha * acc_ref[c]) + pv

    def all_blocks(shifted):
        def body(j, carry):
            kv_block(j, False, shifted)
            return carry
        lax.fori_loop(0, i, body, 0)
        kv_block(i, True, shifted)

    @pl.when(no_shift)
    def _():
        all_blocks(False)

    @pl.when(jnp.logical_not(no_shift))
    def _():
        m_ref[...] = jnp.full_like(m_ref, NEG_BIG)
        all_blocks(True)

    lam = (jnp.exp(jnp.sum(lq1_ref[...] * lk1_ref[...], axis=-1, keepdims=True))
           - jnp.exp(jnp.sum(lq2_ref[...] * lk2_ref[...], axis=-1, keepdims=True))
           + lam_init)
    for c in range(per_map):
        o_t = (acc_ref[c] / l_ref[c]
               - lam * (acc_ref[per_map + c] / l_ref[per_map + c]))
        o = o_t.T
        o = o * lax.rsqrt(jnp.mean(o * o, axis=-1, keepdims=True) + EPS) * gain_ref[...]
        o_ref[c * sub:(c + 1) * sub, :] = (o * (1.0 - lam_init)).astype(o_ref.dtype)


def diff_attention_core(q, k, vt, lq1, lk1, lq2, lk2, sub_gain, lam_init, batch, l_pad):
    rows = q.shape[0]
    t = ROW_TILE
    n_t = l_pad // t
    n_sub = 2 * t // ATT_SUB
    vec = lambda a: a.reshape(1, -1)
    small = lambda n: pl.BlockSpec((1, n), lambda b, h, i: (0, 0))
    return pl.pallas_call(
        functools.partial(_diff_attn_kernel, lam_init=lam_init),
        grid=(batch, DA_HEADS, n_t),
        in_specs=[
            pl.BlockSpec((t, 2 * DA_DH), lambda b, h, i: (b * n_t + i, h)),
            pl.BlockSpec((l_pad, 2 * DA_DH), lambda b, h, i: (b, h)),
            pl.BlockSpec((n_t, DA_DV, t), lambda b, h, i: (b, h, 0)),
            small(DA_DH), small(DA_DH), small(DA_DH), small(DA_DH), small(DA_DV),
        ],
        out_specs=pl.BlockSpec((t, DA_DV), lambda b, h, i: (b * n_t + i, h)),
        out_shape=jax.ShapeDtypeStruct((rows, D_MODEL), BF16),
        scratch_shapes=[
            pltpu.VMEM((n_sub, ATT_SUB, 2 * DA_DH), BF16),
            pltpu.VMEM((n_sub, 1, ATT_SUB), F32),
            pltpu.VMEM((n_sub, 1, ATT_SUB), F32),
            pltpu.VMEM((n_sub, DA_DV, ATT_SUB), F32),
            pltpu.SMEM((1,), F32),
        ],
        compiler_params=_params("arbitrary", "arbitrary", "arbitrary"),
        name="diff_attention",
    )(q, k, vt, vec(lq1), vec(lk1), vec(lq2), vec(lk2), vec(sub_gain))


def kernel(x, meta_tokens, norm_mix, norm_ffn, hg_w_in, hg_lb, hg_gain_o, hg_w_o, kv_norm, w_kv, da_w_q, da_lam_q1, da_lam_k1, da_lam_q2, da_lam_k2, da_sub_gain, da_w_o, ffn_w_up, ffn_conv_w, ffn_conv_b, ffn_w_down, final_norm_g):
    batch, seq, d = x.shape
    n_a = hg_w_in.shape[0]
    depth = norm_mix.shape[0]
    l_real = N_META + seq
    l_pad = -(-l_real // L_ALIGN) * L_ALIGN

    meta = jnp.broadcast_to(meta_tokens[None].astype(x.dtype), (batch, N_META, d))
    tail = jnp.zeros((batch, l_pad - l_real, d), x.dtype)
    h = jnp.concatenate([meta, x, tail], axis=1).reshape(batch * l_pad, d)

    k_sh = vt_sh = None
    for l in range(depth):
        if l < n_a:
            proj = norm_matmul(h, norm_mix[l], hg_w_in[l].astype(BF16), F32)
            o = gla_mixer(proj, hg_lb, hg_gain_o[l], l, batch, l_pad)
            h = matmul_residual(o, hg_w_o[l].astype(BF16), h)
        else:
            j = l - n_a
            lam_init = 0.8 - 0.6 * math.exp(-0.3 * l)
            q = norm_matmul(h, norm_mix[l], da_w_q[j].astype(BF16), BF16,
                            scale=DA_DH ** -0.5 * LOG2E)
            o = diff_attention_core(q, k_sh, vt_sh, da_lam_q1[j], da_lam_k1[j], da_lam_q2[j],
                                    da_lam_k2[j], da_sub_gain[j], lam_init, batch, l_pad)
            h = matmul_residual(o, da_w_o[j].astype(BF16), h)
        gated = ffn_up(h, norm_ffn[l], ffn_w_up[l].astype(BF16), ffn_conv_w[l],
                       ffn_conv_b[l], l_pad)
        h = matmul_residual(gated, ffn_w_down[l].astype(BF16), h,
                            norm_g=final_norm_g if l == depth - 1 else None)
        if l == n_a - 1:
            k_sh = norm_matmul(h, kv_norm, w_kv[:, :DA_QK].astype(BF16), BF16)
            vt_sh = norm_matmul_t(h, kv_norm, w_kv[:, DA_QK:].T.astype(BF16), BF16)

    return h.reshape(batch, l_pad, d)[:, N_META:l_real]
```

```python
import functools
import math

import jax
import jax.numpy as jnp
from jax import lax
from jax.experimental import pallas as pl
from jax.experimental.pallas import tpu as pltpu

D_MODEL = 1024
N_META = 16
HG_HEADS = 8
HG_DK = 128
HG_FDIM = HG_HEADS * HG_DK
F_FLOOR = 1e-30
DA_HEADS = 8
DA_DH = 64
DA_DV = 128
DA_QK = DA_HEADS * 2 * DA_DH
D_FF = 2816
CONV_W = 3
EPS = 1e-5

LANES = 128
MXU_DIM = 256
ROW_TILE = 3 * MXU_DIM
L_ALIGN = ROW_TILE
ATT_SUB = MXU_DIM
LOG2E = 1.4426950408889634
GLA_CHUNK = 64
GLA_SUB = 16
FFN_COL_TILE = 256
PROJ_COL_TILE = 1024
VMEM_LIMIT = 56 * 1024 * 1024
NEG_BIG = -1e30
ATT_NOSHIFT_LIMIT = 48.0
ATT_BLOCKS_PER_ITER = 2
GLA_HEADS_PER_STEP = 2
FFN_ROW_SUB = MXU_DIM
PROJ_ROW_SUB = MXU_DIM

BF16 = jnp.bfloat16
F32 = jnp.float32


def _params(*sem):
    return pltpu.CompilerParams(dimension_semantics=sem, vmem_limit_bytes=VMEM_LIMIT)


def _sigmoid(x):
    return 1.0 / (1.0 + jnp.exp(-x))


def _rms_rows(x, g):
    return x * lax.rsqrt(jnp.mean(x * x, axis=-1, keepdims=True) + EPS) * g


def _norm_matmul_kernel(x_ref, g_ref, w_ref, o_ref, *, scale):
    tm, n = o_ref.shape
    rs = PROJ_ROW_SUB
    tn = min(n, PROJ_COL_TILE)

    def normed(r):
        return _rms_rows(x_ref[r * rs:(r + 1) * rs, :], g_ref[...]).astype(BF16)

    xn_next = normed(0)
    for r in range(tm // rs):
        xn = xn_next
        if (r + 1) * rs < tm:
            xn_next = normed(r + 1)
        for c in range(n // tn):
            y = jnp.dot(xn, w_ref[:, c * tn:(c + 1) * tn], preferred_element_type=F32)
            if scale != 1.0:
                y = y * scale
            o_ref[r * rs:(r + 1) * rs, c * tn:(c + 1) * tn] = y.astype(o_ref.dtype)


def norm_matmul(x, g, w, out_dtype, scale=1.0):
    rows, d = x.shape
    n = w.shape[1]
    return pl.pallas_call(
        functools.partial(_norm_matmul_kernel, scale=scale),
        grid=(rows // ROW_TILE,),
        in_specs=[
            pl.BlockSpec((ROW_TILE, d), lambda i: (i, 0)),
            pl.BlockSpec((1, d), lambda i: (0, 0)),
            pl.BlockSpec((d, n), lambda i: (0, 0), pipeline_mode=pl.Buffered(1)),
        ],
        out_specs=pl.BlockSpec((ROW_TILE, n), lambda i: (i, 0)),
        out_shape=jax.ShapeDtypeStruct((rows, n), out_dtype),
        compiler_params=_params("arbitrary"),
        name="norm_matmul",
    )(x, g.reshape(1, d), w)


def _norm_matmul_t_kernel(x_ref, g_ref, wt_ref, o_ref):
    xn = _rms_rows(x_ref[...], g_ref[...]).astype(BF16)
    o_ref[...] = lax.dot_general(wt_ref[...], xn, (((1,), (1,)), ((), ())),
                                 preferred_element_type=F32).astype(o_ref.dtype)


def norm_matmul_t(x, g, wt, out_dtype):
    rows, d = x.shape
    n = wt.shape[0]
    return pl.pallas_call(
        _norm_matmul_t_kernel,
        grid=(rows // ROW_TILE,),
        in_specs=[
            pl.BlockSpec((ROW_TILE, d), lambda i: (i, 0)),
            pl.BlockSpec((1, d), lambda i: (0, 0)),
            pl.BlockSpec((n, d), lambda i: (0, 0)),
        ],
        out_specs=pl.BlockSpec((None, n, ROW_TILE), lambda i: (i, 0, 0)),
        out_shape=jax.ShapeDtypeStruct((rows // ROW_TILE, n, ROW_TILE), out_dtype),
        compiler_params=_params("arbitrary"),
        name="norm_matmul_t",
    )(x, g.reshape(1, d), wt)


def _blocked_dot(a_ref, w_ref):
    if len(a_ref.shape) == 2:
        return jnp.dot(a_ref[...], w_ref[...], preferred_element_type=F32)
    acc = jnp.dot(a_ref[0], w_ref[0], preferred_element_type=F32)
    for c in range(1, a_ref.shape[0]):
        acc = acc + jnp.dot(a_ref[c], w_ref[c], preferred_element_type=F32)
    return acc


def _matmul_residual_kernel(a_ref, w_ref, h_ref, o_ref):
    o_ref[...] = h_ref[...] + _blocked_dot(a_ref, w_ref)


def _matmul_residual_norm_kernel(a_ref, w_ref, h_ref, g_ref, o_ref):
    o_ref[...] = _rms_rows(h_ref[...] + _blocked_dot(a_ref, w_ref), g_ref[...])


def matmul_residual(a, w, h, norm_g=None):
    n = w.shape[-1]
    rows = a.shape[-2]
    if a.ndim == 2:
        a_spec = pl.BlockSpec((ROW_TILE, a.shape[1]), lambda i: (i, 0))
        w_spec = pl.BlockSpec(w.shape, lambda i: (0, 0), pipeline_mode=pl.Buffered(1))
    else:
        a_spec = pl.BlockSpec((a.shape[0], ROW_TILE, a.shape[2]), lambda i: (0, i, 0))
        w_spec = pl.BlockSpec(w.shape, lambda i: (0, 0, 0), pipeline_mode=pl.Buffered(1))
    in_specs = [a_spec, w_spec, pl.BlockSpec((ROW_TILE, n), lambda i: (i, 0))]
    args = (a, w, h)
    body = _matmul_residual_kernel
    if norm_g is not None:
        in_specs.append(pl.BlockSpec((1, n), lambda i: (0, 0)))
        args += (norm_g.reshape(1, n),)
        body = _matmul_residual_norm_kernel
    return pl.pallas_call(
        body,
        grid=(rows // ROW_TILE,),
        in_specs=in_specs,
        out_specs=pl.BlockSpec((ROW_TILE, n), lambda i: (i, 0)),
        out_shape=jax.ShapeDtypeStruct((rows, n), F32),
        compiler_params=_params("arbitrary"),
        name="matmul_residual",
    )(*args)


def _gla_kernel(q_ref, f_ref, i_ref, g_ref, lb_ref, gain_ref, o_ref,
                st_ref, b_scr, k_scr, v_scr, *, layer, n_chunks):
    C, SUB, HALF = GLA_CHUNK, GLA_SUB, GLA_SUB // 2
    n_sub = C // SUB
    heads = range(GLA_HEADS_PER_STEP)
    lanes = [slice(h * HG_DK, (h + 1) * HG_DK) for h in heads]

    @pl.when(pl.program_id(2) == 0)
    def _():
        st_ref[...] = jnp.zeros_like(st_ref)

    lbp = lb_ref[...]
    e = jnp.exp(lbp - jnp.max(lbp, axis=0, keepdims=True))
    p = e / jnp.sum(e, axis=0, keepdims=True)
    lb_all = jnp.sum(p[:layer + 1], axis=0, keepdims=True) - p[0:1]
    gain_all = gain_ref[...]

    row = lax.broadcasted_iota(jnp.int32, (C, C), 0)
    col = lax.broadcasted_iota(jnp.int32, (C, C), 1)
    tril = (row >= col).astype(BF16)
    half_row = lax.broadcasted_iota(jnp.int32, (HALF, 1), 0)
    sub_col = lax.broadcasted_iota(jnp.int32, (SUB, C), 1)
    nt = (((1,), (1,)), ((), ()))

    def chunk(c, carry):
        r0 = pl.multiple_of(c * C, C)
        rows = pl.ds(r0, C)
        q, k, v16, b = [], [], [], []
        for h in heads:
            lb = lb_all[:, lanes[h]]
            qh = q_ref[rows, lanes[h]]
            q.append(qh * _sigmoid(qh))
            sig = _sigmoid(f_ref[rows, lanes[h]])
            lf = jnp.log(jnp.maximum(lb + (1.0 - lb) * sig, F_FLOOR))
            k.append((1.0 - lb) * (1.0 - sig))
            v = i_ref[rows, lanes[h]]
            v_scr[h] = v
            v16.append(v.astype(BF16))
            lf_hi = lf.astype(BF16)
            rem = lf - lf_hi.astype(F32)
            lf_mid = rem.astype(BF16)
            lf_lo = (rem - lf_mid.astype(F32)).astype(BF16)
            b.append(jnp.dot(tril, lf_hi, preferred_element_type=F32)
                     + jnp.dot(tril, lf_mid, preferred_element_type=F32)
                     + jnp.dot(tril, lf_lo, preferred_element_type=F32))

        st, o_inter, a_off, kv, decay = [], [], [], [], []
        for h in heads:
            b_scr[h] = b[h]
            k_scr[h] = k[h]
            b_last = b_scr[h, C - 1:C, :]
            st.append(st_ref[h])
            qb = (q[h] * jnp.exp(b[h])).astype(BF16)
            o_inter.append(lax.dot_general(qb, st[h].astype(BF16), nt,
                                           preferred_element_type=F32))
            kd = (k[h] * jnp.exp(b_last - b[h])).astype(BF16)
            kv.append(lax.dot_general(v16[h], kd, (((0,), (0,)), ((), ())),
                                      preferred_element_type=F32))
            decay.append(jnp.exp(b_last))
            a_rows = [jnp.zeros((SUB, C), F32)]
            for i in range(1, n_sub):
                lo = i * SUB
                r_i = b_scr[h, lo - 1:lo, :]
                q_s = (q[h][lo:lo + SUB] * jnp.exp(b[h][lo:lo + SUB] - r_i)).astype(BF16)
                k_s = (k[h] * jnp.exp(jnp.minimum(r_i - b[h], 0.0))).astype(BF16)
                a_i = lax.dot_general(q_s, k_s, nt, preferred_element_type=F32)
                a_rows.append(jnp.where(sub_col < lo, a_i, 0.0))
            a_off.append(jnp.concatenate(a_rows, axis=0).astype(BF16))

        o_diag = []
        for h in heads:
            pieces = []
            for i in range(n_sub):
                lo = i * SUB
                q_lo, q_hi = q[h][lo:lo + HALF], q[h][lo + HALF:lo + SUB]
                b_lo, b_hi = b[h][lo:lo + HALF], b[h][lo + HALF:lo + SUB]
                acc_lo = jnp.zeros((HALF, HG_DK), F32)
                acc_hi = jnp.zeros((HALF, HG_DK), F32)
                for s in range(SUB):
                    b_s = b_scr[h, lo + s:lo + s + 1, :]
                    k_row = k_scr[h, lo + s:lo + s + 1, :]
                    v_row = v_scr[h, lo + s:lo + s + 1, :]
                    w_hi = jnp.sum(q_hi * k_row * jnp.exp(b_hi - b_s), axis=-1, keepdims=True)
                    if s < HALF:
                        w_lo = jnp.sum(q_lo * k_row * jnp.exp(b_lo - b_s), axis=-1,
                                       keepdims=True)
                        acc_lo = acc_lo + jnp.where(half_row >= s, w_lo, 0.0) * v_row
                        acc_hi = acc_hi + w_hi * v_row
                    else:
                        acc_hi = acc_hi + jnp.where(half_row >= s - HALF, w_hi, 0.0) * v_row
                pieces += [acc_lo, acc_hi]
            o_diag.append(jnp.concatenate(pieces, axis=0))

        for h in heads:
            o = (o_inter[h] + jnp.dot(a_off[h], v16[h], preferred_element_type=F32)
                 + o_diag[h])
            st_ref[h] = st[h] * decay[h] + kv[h]
            g = g_ref[rows, lanes[h]]
            o = o * lax.rsqrt(jnp.mean(o * o, axis=-1, keepdims=True) + EPS)
            o_ref[rows, lanes[h]] = (o * gain_all[:, lanes[h]] * (g * _sigmoid(g))
                                     ).astype(o_ref.dtype)
        return carry

    lax.fori_loop(0, n_chunks, chunk, 0, unroll=2)


def gla_mixer(proj, hg_lb, gain, layer, batch, l_pad):
    rows = proj.shape[0]
    n_t = l_pad // ROW_TILE
    n_layers = hg_lb.shape[0]
    hps = GLA_HEADS_PER_STEP
    width = hps * HG_DK
    groups = HG_HEADS // hps
    col = lambda part: (lambda b, h, t: (b * n_t + t, part * groups + h))
    return pl.pallas_call(
        functools.partial(_gla_kernel, layer=layer, n_chunks=ROW_TILE // GLA_CHUNK),
        grid=(batch, groups, n_t),
        in_specs=[
            pl.BlockSpec((ROW_TILE, width), col(0)),
            pl.BlockSpec((ROW_TILE, width), col(1)),
            pl.BlockSpec((ROW_TILE, width), col(2)),
            pl.BlockSpec((ROW_TILE, width), col(3)),
            pl.BlockSpec((n_layers, width), lambda b, h, t: (0, h)),
            pl.BlockSpec((1, width), lambda b, h, t: (0, h)),
        ],
        out_specs=pl.BlockSpec((ROW_TILE, width), col(0)),
        out_shape=jax.ShapeDtypeStruct((rows, D_MODEL), BF16),
        scratch_shapes=[
            pltpu.VMEM((hps, HG_DK, HG_DK), F32),
            pltpu.VMEM((hps, GLA_CHUNK, HG_DK), F32),
            pltpu.VMEM((hps, GLA_CHUNK, HG_DK), F32),
            pltpu.VMEM((hps, GLA_CHUNK, HG_DK), F32),
        ],
        compiler_params=_params("arbitrary", "arbitrary", "arbitrary"),
        name="gla_mixer",
    )(proj, proj, proj, proj, hg_lb, gain.reshape(1, D_MODEL))


def _ffn_up_kernel(x_ref, g_ref, w_ref, cw_ref, cb_ref, o_ref, xn_ref, carry_ref,
                   *, tiles_per_seq):
    i = pl.program_id(0)
    tm = xn_ref.shape[0]
    rs = FFN_ROW_SUB
    n_cb = o_ref.shape[0]
    halo = carry_ref.shape[1]

    xn_ref[...] = _rms_rows(x_ref[...], g_ref[...]).astype(BF16)

    @pl.when((i % tiles_per_seq) == 0)
    def _():
        carry_ref[...] = jnp.zeros_like(carry_ref)

    def project(c, r):
        xr = xn_ref[r * rs:(r + 1) * rs, :]
        return (jnp.dot(xr, w_ref[c], preferred_element_type=F32),
                jnp.dot(xr, w_ref[n_cb + c], preferred_element_type=F32))

    def conv(tail, u, cb):
        ue = jnp.concatenate([tail, u], axis=0)
        u1 = pltpu.roll(ue, 1, axis=0)[halo:]
        u2 = pltpu.roll(ue, 2, axis=0)[halo:]
        cw = cw_ref[cb]
        return u2 * cw[0:1, :] + u1 * cw[1:2, :] + u * cw[2:3, :] + cb_ref[cb]

    stages = [(c, r) for c in range(n_cb) for r in range(tm // rs)]
    u_next = project(*stages[0])
    tail_a = tail_b = None
    for idx, (c, r) in enumerate(stages):
        ua, ub = u_next
        if idx + 1 < len(stages):
            u_next = project(*stages[idx + 1])
        if r == 0:
            tail_a, tail_b = carry_ref[c], carry_ref[n_cb + c]
        a = conv(tail_a, ua, c)
        b = conv(tail_b, ub, n_cb + c)
        tail_a, tail_b = ua[rs - halo:, :], ub[rs - halo:, :]
        if (r + 1) * rs == tm:
            carry_ref[c] = tail_a
            carry_ref[n_cb + c] = tail_b
        o_ref[c, r * rs:(r + 1) * rs, :] = (a * _sigmoid(a) * b).astype(o_ref.dtype)


def ffn_up(h, g, w_up, conv_w, conv_b, l_pad):
    rows, d = h.shape
    tn = FFN_COL_TILE
    n_cb = D_FF // tn
    w3 = w_up.reshape(d, 2 * n_cb, tn).transpose(1, 0, 2)
    cw3 = conv_w.reshape(CONV_W, 2 * n_cb, tn).transpose(1, 0, 2)
    cb3 = conv_b.reshape(2 * n_cb, 1, tn)
    whole = lambda a: pl.BlockSpec(a.shape, lambda i: (0,) * a.ndim,
                                   pipeline_mode=pl.Buffered(1))
    return pl.pallas_call(
        functools.partial(_ffn_up_kernel, tiles_per_seq=l_pad // ROW_TILE),
        grid=(rows // ROW_TILE,),
        in_specs=[
            pl.BlockSpec((ROW_TILE, d), lambda i: (i, 0)),
            pl.BlockSpec((1, d), lambda i: (0, 0)),
            whole(w3), whole(cw3), whole(cb3),
        ],
        out_specs=pl.BlockSpec((n_cb, ROW_TILE, tn), lambda i: (0, i, 0)),
        out_shape=jax.ShapeDtypeStruct((n_cb, rows, tn), BF16),
        scratch_shapes=[
            pltpu.VMEM((ROW_TILE, d), BF16),
            pltpu.VMEM((2 * n_cb, 8, tn), F32),
        ],
        compiler_params=_params("arbitrary"),
        name="ffn_up",
    )(h, g.reshape(1, d), w3, cw3, cb3)


def _diff_attn_kernel(q_ref, k_ref, vt_ref, lq1_ref, lk1_ref, lq2_ref, lk2_ref, gain_ref,
                      o_ref, qs_ref, m_ref, l_ref, acc_ref, kmax_ref, *, lam_init):
    i = pl.program_id(2)
    tq = q_ref.shape[0]
    tk = tq
    n_t = vt_ref.shape[0]
    sub = ATT_SUB
    per_map = tq // sub
    n_sub = 2 * per_map

    @pl.when(i == 0)
    def _():
        def body(t, mx):
            kb = k_ref[pl.ds(pl.multiple_of(t * tk, tk), tk), :].astype(F32)
            return jnp.maximum(mx, jnp.max(jnp.sum(kb * kb, axis=-1, keepdims=True)))
        kmax_ref[0] = lax.fori_loop(0, n_t, body, jnp.float32(0.0))

    q = q_ref[...]
    lane = lax.broadcasted_iota(jnp.int32, (sub, 2 * DA_DH), 1)
    for c in range(per_map):
        q_c = q[c * sub:(c + 1) * sub, :]
        qs_ref[c] = jnp.where(lane < DA_DH, q_c, jnp.zeros_like(q_c))
        qs_ref[per_map + c] = jnp.where(lane >= DA_DH, q_c, jnp.zeros_like(q_c))
    l_ref[...] = jnp.zeros_like(l_ref)
    acc_ref[...] = jnp.zeros_like(acc_ref)
    qf = q.astype(F32)
    qmax = jnp.max(jnp.sum(qf * qf, axis=-1, keepdims=True))
    no_shift = qmax * kmax_ref[0] <= ATT_NOSHIFT_LIMIT ** 2

    def kv_blocks(blocks, shifted):
        def n_keys(masked, c):
            return ((c % per_map) + 1) * sub if masked else tk

        def scores(j, masked, c):
            row0 = pl.multiple_of(j * tk, tk)
            return lax.dot_general(k_ref[pl.ds(row0, n_keys(masked, c)), :], qs_ref[c],
                                   (((1,), (1,)), ((), ())),
                                   preferred_element_type=F32)

        def softmax(masked, c, s):
            if masked:
                key = lax.broadcasted_iota(jnp.int32, s.shape, 0)
                qry = lax.broadcasted_iota(jnp.int32, s.shape, 1) + (c % per_map) * sub
                s = jnp.where(key <= qry, s, NEG_BIG)
            if not shifted:
                p = jnp.exp2(s)
                l_ref[c] = l_ref[c] + jnp.sum(p, axis=0, keepdims=True)
                return None, p.astype(BF16)
            m_prev = m_ref[c]
            m_new = jnp.maximum(m_prev, jnp.max(s, axis=0, keepdims=True))
            alpha = jnp.exp2(m_prev - m_new)
            p = jnp.exp2(s - m_new)
            l_ref[c] = alpha * l_ref[c] + jnp.sum(p, axis=0, keepdims=True)
            m_ref[c] = m_new
            return alpha, p.astype(BF16)

        stages = [(j, masked, c) for j, masked in blocks for c in range(n_sub)]
        s_q = [scores(*stages[0]), scores(*stages[1])]
        ap = softmax(*stages[0][1:], s_q.pop(0))
        for g, (j, masked, c) in enumerate(stages):
            if g + 2 < len(stages):
                s_q.append(scores(*stages[g + 2]))
            alpha, p = ap
            if g + 1 < len(stages):
                ap = softmax(*stages[g + 1][1:], s_q.pop(0))
            pv = jnp.dot(vt_ref[j, :, pl.ds(0, n_keys(masked, c))], p,
                         preferred_element_type=F32)
            acc_ref[c] = (acc_ref[c] if alpha is None else alpha * acc_ref[c]) + pv

    def all_blocks(shifted, per_iter):
        def body(t, carry):
            kv_blocks([(t * per_iter + u, False) for u in range(per_iter)], shifted)
            return carry
        lax.fori_loop(0, i // per_iter, body, 0)
        for rem in range(per_iter):
            @pl.when(i % per_iter == rem)
            def _():
                kv_blocks([(i - rem + u, False) for u in range(rem)] + [(i, True)], shifted)

    @pl.when(no_shift)
    def _():
        all_blocks(False, ATT_BLOCKS_PER_ITER)

    @pl.when(jnp.logical_not(no_shift))
    def _():
        m_ref[...] = jnp.full_like(m_ref, NEG_BIG)
        all_blocks(True, 1)

    lam = (jnp.exp(jnp.sum(lq1_ref[...] * lk1_ref[...], axis=-1, keepdims=True))
           - jnp.exp(jnp.sum(lq2_ref[...] * lk2_ref[...], axis=-1, keepdims=True))
           + lam_init)
    for c in range(per_map):
        o_t = (acc_ref[c] / l_ref[c]
               - lam * (acc_ref[per_map + c] / l_ref[per_map + c]))
        o = o_t.T
        o = o * lax.rsqrt(jnp.mean(o * o, axis=-1, keepdims=True) + EPS) * gain_ref[...]
        o_ref[c * sub:(c + 1) * sub, :] = (o * (1.0 - lam_init)).astype(o_ref.dtype)


def diff_attention_core(q, k, vt, lq1, lk1, lq2, lk2, sub_gain, lam_init, batch, l_pad):
    rows = q.shape[0]
    t = ROW_TILE
    n_t = l_pad // t
    n_sub = 2 * t // ATT_SUB
    vec = lambda a: a.reshape(1, -1)
    small = lambda n: pl.BlockSpec((1, n), lambda b, h, i: (0, 0))
    return pl.pallas_call(
        functools.partial(_diff_attn_kernel, lam_init=lam_init),
        grid=(batch, DA_HEADS, n_t),
        in_specs=[
            pl.BlockSpec((t, 2 * DA_DH), lambda b, h, i: (b * n_t + i, h)),
            pl.BlockSpec((l_pad, 2 * DA_DH), lambda b, h, i: (b, h)),
            pl.BlockSpec((n_t, DA_DV, t), lambda b, h, i: (b, h, 0)),
            small(DA_DH), small(DA_DH), small(DA_DH), small(DA_DH), small(DA_DV),
        ],
        out_specs=pl.BlockSpec((t, DA_DV), lambda b, h, i: (b * n_t + i, h)),
        out_shape=jax.ShapeDtypeStruct((rows, D_MODEL), BF16),
        scratch_shapes=[
            pltpu.VMEM((n_sub, ATT_SUB, 2 * DA_DH), BF16),
            pltpu.VMEM((n_sub, 1, ATT_SUB), F32),
            pltpu.VMEM((n_sub, 1, ATT_SUB), F32),
            pltpu.VMEM((n_sub, DA_DV, ATT_SUB), F32),
            pltpu.SMEM((1,), F32),
        ],
        compiler_params=_params("arbitrary", "arbitrary", "arbitrary"),
        name="diff_attention",
    )(q, k, vt, vec(lq1), vec(lk1), vec(lq2), vec(lk2), vec(sub_gain))


def kernel(x, meta_tokens, norm_mix, norm_ffn, hg_w_in, hg_lb, hg_gain_o, hg_w_o, kv_norm, w_kv, da_w_q, da_lam_q1, da_lam_k1, da_lam_q2, da_lam_k2, da_sub_gain, da_w_o, ffn_w_up, ffn_conv_w, ffn_conv_b, ffn_w_down, final_norm_g):
    batch, seq, d = x.shape
    n_a = hg_w_in.shape[0]
    depth = norm_mix.shape[0]
    l_real = N_META + seq
    l_pad = -(-l_real // L_ALIGN) * L_ALIGN

    meta = jnp.broadcast_to(meta_tokens[None].astype(x.dtype), (batch, N_META, d))
    tail = jnp.zeros((batch, l_pad - l_real, d), x.dtype)
    h = jnp.concatenate([meta, x, tail], axis=1).reshape(batch * l_pad, d)

    k_sh = vt_sh = None
    for l in range(depth):
        if l < n_a:
            proj = norm_matmul(h, norm_mix[l], hg_w_in[l].astype(BF16), F32)
            o = gla_mixer(proj, hg_lb, hg_gain_o[l], l, batch, l_pad)
            h = matmul_residual(o, hg_w_o[l].astype(BF16), h)
        else:
            j = l - n_a
            lam_init = 0.8 - 0.6 * math.exp(-0.3 * l)
            q = norm_matmul(h, norm_mix[l], da_w_q[j].astype(BF16), BF16,
                            scale=DA_DH ** -0.5 * LOG2E)
            o = diff_attention_core(q, k_sh, vt_sh, da_lam_q1[j], da_lam_k1[j], da_lam_q2[j],
                                    da_lam_k2[j], da_sub_gain[j], lam_init, batch, l_pad)
            h = matmul_residual(o, da_w_o[j].astype(BF16), h)
        gated = ffn_up(h, norm_ffn[l], ffn_w_up[l].astype(BF16), ffn_conv_w[l],
                       ffn_conv_b[l], l_pad)
        w_down = ffn_w_down[l].astype(BF16).reshape(gated.shape[0], gated.shape[2], d)
        h = matmul_residual(gated, w_down, h,
                            norm_g=final_norm_g if l == depth - 1 else None)
        if l == n_a - 1:
            k_sh = norm_matmul(h, kv_norm, w_kv[:, :DA_QK].astype(BF16), BF16)
            vt_sh = norm_matmul_t(h, kv_norm, w_kv[:, DA_QK:].T.astype(BF16), BF16)

    return h.reshape(batch, l_pad, d)[:, N_META:l_real]
```

```python
import functools
import math

import jax
import jax.numpy as jnp
from jax import lax
from jax.experimental import pallas as pl
from jax.experimental.pallas import tpu as pltpu

D_MODEL = 1024
N_META = 16
HG_HEADS = 8
HG_DK = 128
HG_FDIM = HG_HEADS * HG_DK
F_FLOOR = 1e-30
DA_HEADS = 8
DA_DH = 64
DA_DV = 128
DA_QK = DA_HEADS * 2 * DA_DH
D_FF = 2816
CONV_W = 3
EPS = 1e-5

LANES = 128
MXU_DIM = 256
ROW_TILE = 3 * MXU_DIM
L_ALIGN = ROW_TILE
ATT_SUB = MXU_DIM
LOG2E = 1.4426950408889634
GLA_CHUNK = 64
GLA_SUB = 16
FFN_COL_TILE = 256
PROJ_COL_TILE = 1024
VMEM_LIMIT = 56 * 1024 * 1024
NEG_BIG = -1e30
ATT_NOSHIFT_LIMIT = 48.0
ATT_BLOCKS_PER_ITER = 3
GLA_HEADS_PER_STEP = 8
FFN_ROW_SUB = MXU_DIM
PROJ_ROW_SUB = MXU_DIM

BF16 = jnp.bfloat16
F32 = jnp.float32


def _params(*sem):
    return pltpu.CompilerParams(dimension_semantics=sem, vmem_limit_bytes=VMEM_LIMIT)


def _sigmoid(x):
    return 1.0 / (1.0 + jnp.exp(-x))


def _rms_rows(x, g):
    return x * lax.rsqrt(jnp.mean(x * x, axis=-1, keepdims=True) + EPS) * g


def _norm_matmul_kernel(x_ref, g_ref, w_ref, o_ref, *, scale):
    tm, n = o_ref.shape
    rs = PROJ_ROW_SUB
    tn = min(n, PROJ_COL_TILE)

    def normed(r):
        return _rms_rows(x_ref[r * rs:(r + 1) * rs, :], g_ref[...]).astype(BF16)

    xn_next = normed(0)
    for r in range(tm // rs):
        xn = xn_next
        if (r + 1) * rs < tm:
            xn_next = normed(r + 1)
        for c in range(n // tn):
            y = jnp.dot(xn, w_ref[:, c * tn:(c + 1) * tn], preferred_element_type=F32)
            if scale != 1.0:
                y = y * scale
            o_ref[r * rs:(r + 1) * rs, c * tn:(c + 1) * tn] = y.astype(o_ref.dtype)


def norm_matmul(x, g, w, out_dtype, scale=1.0):
    rows, d = x.shape
    n = w.shape[1]
    return pl.pallas_call(
        functools.partial(_norm_matmul_kernel, scale=scale),
        grid=(rows // ROW_TILE,),
        in_specs=[
            pl.BlockSpec((ROW_TILE, d), lambda i: (i, 0)),
            pl.BlockSpec((1, d), lambda i: (0, 0)),
            pl.BlockSpec((d, n), lambda i: (0, 0), pipeline_mode=pl.Buffered(1)),
        ],
        out_specs=pl.BlockSpec((ROW_TILE, n), lambda i: (i, 0)),
        out_shape=jax.ShapeDtypeStruct((rows, n), out_dtype),
        compiler_params=_params("arbitrary"),
        name="norm_matmul",
    )(x, g.reshape(1, d), w)


def _norm_matmul_t_kernel(x_ref, g_ref, wt_ref, o_ref):
    xn = _rms_rows(x_ref[...], g_ref[...]).astype(BF16)
    o_ref[...] = lax.dot_general(wt_ref[...], xn, (((1,), (1,)), ((), ())),
                                 preferred_element_type=F32).astype(o_ref.dtype)


def norm_matmul_t(x, g, wt, out_dtype):
    rows, d = x.shape
    n = wt.shape[0]
    return pl.pallas_call(
        _norm_matmul_t_kernel,
        grid=(rows // ROW_TILE,),
        in_specs=[
            pl.BlockSpec((ROW_TILE, d), lambda i: (i, 0)),
            pl.BlockSpec((1, d), lambda i: (0, 0)),
            pl.BlockSpec((n, d), lambda i: (0, 0)),
        ],
        out_specs=pl.BlockSpec((None, n, ROW_TILE), lambda i: (i, 0, 0)),
        out_shape=jax.ShapeDtypeStruct((rows // ROW_TILE, n, ROW_TILE), out_dtype),
        compiler_params=_params("arbitrary"),
        name="norm_matmul_t",
    )(x, g.reshape(1, d), wt)


def _blocked_dot(a_ref, w_ref):
    if len(a_ref.shape) == 2:
        return jnp.dot(a_ref[...], w_ref[...], preferred_element_type=F32)
    acc = jnp.dot(a_ref[0], w_ref[0], preferred_element_type=F32)
    for c in range(1, a_ref.shape[0]):
        acc = acc + jnp.dot(a_ref[c], w_ref[c], preferred_element_type=F32)
    return acc


def _matmul_residual_kernel(a_ref, w_ref, h_ref, o_ref):
    o_ref[...] = h_ref[...] + _blocked_dot(a_ref, w_ref)


def _matmul_residual_norm_kernel(a_ref, w_ref, h_ref, g_ref, o_ref):
    o_ref[...] = _rms_rows(h_ref[...] + _blocked_dot(a_ref, w_ref), g_ref[...])


def matmul_residual(a, w, h, norm_g=None):
    n = w.shape[-1]
    rows = a.shape[-2]
    if a.ndim == 2:
        a_spec = pl.BlockSpec((ROW_TILE, a.shape[1]), lambda i: (i, 0))
        w_spec = pl.BlockSpec(w.shape, lambda i: (0, 0), pipeline_mode=pl.Buffered(1))
    else:
        a_spec = pl.BlockSpec((a.shape[0], ROW_TILE, a.shape[2]), lambda i: (0, i, 0))
        w_spec = pl.BlockSpec(w.shape, lambda i: (0, 0, 0), pipeline_mode=pl.Buffered(1))
    in_specs = [a_spec, w_spec, pl.BlockSpec((ROW_TILE, n), lambda i: (i, 0))]
    args = (a, w, h)
    body = _matmul_residual_kernel
    if norm_g is not None:
        in_specs.append(pl.BlockSpec((1, n), lambda i: (0, 0)))
        args += (norm_g.reshape(1, n),)
        body = _matmul_residual_norm_kernel
    return pl.pallas_call(
        body,
        grid=(rows // ROW_TILE,),
        in_specs=in_specs,
        out_specs=pl.BlockSpec((ROW_TILE, n), lambda i: (i, 0)),
        out_shape=jax.ShapeDtypeStruct((rows, n), F32),
        compiler_params=_params("arbitrary"),
        name="matmul_residual",
    )(*args)


def _gla_kernel(q_ref, f_ref, i_ref, g_ref, lb_ref, gain_ref, o_ref,
                st_ref, b_scr, k_scr, v_scr, *, layer, n_chunks):
    C, SUB, HALF = GLA_CHUNK, GLA_SUB, GLA_SUB // 2
    n_sub = C // SUB
    heads = range(GLA_HEADS_PER_STEP)
    lanes = [slice(h * HG_DK, (h + 1) * HG_DK) for h in heads]

    @pl.when(pl.program_id(2) == 0)
    def _():
        st_ref[...] = jnp.zeros_like(st_ref)

    lbp = lb_ref[...]
    e = jnp.exp(lbp - jnp.max(lbp, axis=0, keepdims=True))
    p = e / jnp.sum(e, axis=0, keepdims=True)
    lb_all = jnp.sum(p[:layer + 1], axis=0, keepdims=True) - p[0:1]
    gain_all = gain_ref[...]

    row = lax.broadcasted_iota(jnp.int32, (C, C), 0)
    col = lax.broadcasted_iota(jnp.int32, (C, C), 1)
    tril = (row >= col).astype(BF16)
    half_row = lax.broadcasted_iota(jnp.int32, (HALF, 1), 0)
    sub_col = lax.broadcasted_iota(jnp.int32, (SUB, C), 1)
    nt = (((1,), (1,)), ((), ()))

    def chunk(c, carry):
        r0 = pl.multiple_of(c * C, C)
        rows = pl.ds(r0, C)
        q, k, v16, b = [], [], [], []
        for h in heads:
            lb = lb_all[:, lanes[h]]
            qh = q_ref[rows, lanes[h]]
            q.append(qh * _sigmoid(qh))
            sig = _sigmoid(f_ref[rows, lanes[h]])
            lf = jnp.log(jnp.maximum(lb + (1.0 - lb) * sig, F_FLOOR))
            k.append((1.0 - lb) * (1.0 - sig))
            v = i_ref[rows, lanes[h]]
            v_scr[h] = v
            v16.append(v.astype(BF16))
            lf_hi = lf.astype(BF16)
            rem = lf - lf_hi.astype(F32)
            lf_mid = rem.astype(BF16)
            lf_lo = (rem - lf_mid.astype(F32)).astype(BF16)
            b.append(jnp.dot(tril, lf_hi, preferred_element_type=F32)
                     + jnp.dot(tril, lf_mid, preferred_element_type=F32)
                     + jnp.dot(tril, lf_lo, preferred_element_type=F32))

        st, o_inter, a_off, kv, decay = [], [], [], [], []
        for h in heads:
            b_scr[h] = b[h]
            k_scr[h] = k[h]
            b_last = b_scr[h, C - 1:C, :]
            st.append(st_ref[h])
            qb = (q[h] * jnp.exp(b[h])).astype(BF16)
            o_inter.append(lax.dot_general(qb, st[h].astype(BF16), nt,
                                           preferred_element_type=F32))
            kd = (k[h] * jnp.exp(b_last - b[h])).astype(BF16)
            kv.append(lax.dot_general(v16[h], kd, (((0,), (0,)), ((), ())),
                                      preferred_element_type=F32))
            decay.append(jnp.exp(b_last))
            a_rows = [jnp.zeros((SUB, C), F32)]
            for i in range(1, n_sub):
                lo = i * SUB
                r_i = b_scr[h, lo - 1:lo, :]
                q_s = (q[h][lo:lo + SUB] * jnp.exp(b[h][lo:lo + SUB] - r_i)).astype(BF16)
                k_s = (k[h] * jnp.exp(jnp.minimum(r_i - b[h], 0.0))).astype(BF16)
                a_i = lax.dot_general(q_s, k_s, nt, preferred_element_type=F32)
                a_rows.append(jnp.where(sub_col < lo, a_i, 0.0))
            a_off.append(jnp.concatenate(a_rows, axis=0).astype(BF16))

        o_diag = []
        for h in heads:
            pieces = []
            for i in range(n_sub):
                lo = i * SUB
                q_lo, q_hi = q[h][lo:lo + HALF], q[h][lo + HALF:lo + SUB]
                b_lo, b_hi = b[h][lo:lo + HALF], b[h][lo + HALF:lo + SUB]
                acc_lo = jnp.zeros((HALF, HG_DK), F32)
                acc_hi = jnp.zeros((HALF, HG_DK), F32)
                for s in range(SUB):
                    b_s = b_scr[h, lo + s:lo + s + 1, :]
                    k_row = k_scr[h, lo + s:lo + s + 1, :]
                    v_row = v_scr[h, lo + s:lo + s + 1, :]
                    w_hi = jnp.sum(q_hi * k_row * jnp.exp(b_hi - b_s), axis=-1, keepdims=True)
                    if s < HALF:
                        w_lo = jnp.sum(q_lo * k_row * jnp.exp(b_lo - b_s), axis=-1,
                                       keepdims=True)
                        acc_lo = acc_lo + jnp.where(half_row >= s, w_lo, 0.0) * v_row
                        acc_hi = acc_hi + w_hi * v_row
                    else:
                        acc_hi = acc_hi + jnp.where(half_row >= s - HALF, w_hi, 0.0) * v_row
                pieces += [acc_lo, acc_hi]
            o_diag.append(jnp.concatenate(pieces, axis=0))

        for h in heads:
            o = (o_inter[h] + jnp.dot(a_off[h], v16[h], preferred_element_type=F32)
                 + o_diag[h])
            st_ref[h] = st[h] * decay[h] + kv[h]
            g = g_ref[rows, lanes[h]]
            o = o * lax.rsqrt(jnp.mean(o * o, axis=-1, keepdims=True) + EPS)
            o_ref[rows, lanes[h]] = (o * gain_all[:, lanes[h]] * (g * _sigmoid(g))
                                     ).astype(o_ref.dtype)
        return carry

    lax.fori_loop(0, n_chunks, chunk, 0)


def gla_mixer(proj, hg_lb, gain, layer, batch, l_pad):
    rows = proj.shape[0]
    n_t = l_pad // ROW_TILE
    n_layers = hg_lb.shape[0]
    hps = GLA_HEADS_PER_STEP
    width = hps * HG_DK
    groups = HG_HEADS // hps
    col = lambda part: (lambda b, h, t: (b * n_t + t, part * groups + h))
    return pl.pallas_call(
        functools.partial(_gla_kernel, layer=layer, n_chunks=ROW_TILE // GLA_CHUNK),
        grid=(batch, groups, n_t),
        in_specs=[
            pl.BlockSpec((ROW_TILE, width), col(0)),
            pl.BlockSpec((ROW_TILE, width), col(1)),
            pl.BlockSpec((ROW_TILE, width), col(2)),
            pl.BlockSpec((ROW_TILE, width), col(3)),
            pl.BlockSpec((n_layers, width), lambda b, h, t: (0, h)),
            pl.BlockSpec((1, width), lambda b, h, t: (0, h)),
        ],
        out_specs=pl.BlockSpec((ROW_TILE, width), col(0)),
        out_shape=jax.ShapeDtypeStruct((rows, D_MODEL), BF16),
        scratch_shapes=[
            pltpu.VMEM((hps, HG_DK, HG_DK), F32),
            pltpu.VMEM((hps, GLA_CHUNK, HG_DK), F32),
            pltpu.VMEM((hps, GLA_CHUNK, HG_DK), F32),
            pltpu.VMEM((hps, GLA_CHUNK, HG_DK), F32),
        ],
        compiler_params=_params("arbitrary", "arbitrary", "arbitrary"),
        name="gla_mixer",
    )(proj, proj, proj, proj, hg_lb, gain.reshape(1, D_MODEL))


def _ffn_up_kernel(x_ref, g_ref, w_ref, cw_ref, cb_ref, o_ref, xn_ref, carry_ref,
                   *, tiles_per_seq):
    i = pl.program_id(0)
    tm = xn_ref.shape[0]
    rs = FFN_ROW_SUB
    n_cb = o_ref.shape[0]
    halo = carry_ref.shape[1]

    xn_ref[...] = _rms_rows(x_ref[...], g_ref[...]).astype(BF16)

    @pl.when((i % tiles_per_seq) == 0)
    def _():
        carry_ref[...] = jnp.zeros_like(carry_ref)

    def project(c, r):
        xr = xn_ref[r * rs:(r + 1) * rs, :]
        return (jnp.dot(xr, w_ref[c], preferred_element_type=F32),
                jnp.dot(xr, w_ref[n_cb + c], preferred_element_type=F32))

    def conv(tail, u, cb):
        ue = jnp.concatenate([tail, u], axis=0)
        u1 = pltpu.roll(ue, 1, axis=0)[halo:]
        u2 = pltpu.roll(ue, 2, axis=0)[halo:]
        cw = cw_ref[cb]
        return u2 * cw[0:1, :] + u1 * cw[1:2, :] + u * cw[2:3, :] + cb_ref[cb]

    stages = [(c, r) for c in range(n_cb) for r in range(tm // rs)]
    u_next = project(*stages[0])
    tail_a = tail_b = None
    for idx, (c, r) in enumerate(stages):
        ua, ub = u_next
        if idx + 1 < len(stages):
            u_next = project(*stages[idx + 1])
        if r == 0:
            tail_a, tail_b = carry_ref[c], carry_ref[n_cb + c]
        a = conv(tail_a, ua, c)
        b = conv(tail_b, ub, n_cb + c)
        tail_a, tail_b = ua[rs - halo:, :], ub[rs - halo:, :]
        if (r + 1) * rs == tm:
            carry_ref[c] = tail_a
            carry_ref[n_cb + c] = tail_b
        o_ref[c, r * rs:(r + 1) * rs, :] = (a * _sigmoid(a) * b).astype(o_ref.dtype)


def ffn_up(h, g, w_up, conv_w, conv_b, l_pad):
    rows, d = h.shape
    tn = FFN_COL_TILE
    n_cb = D_FF // tn
    w3 = w_up.reshape(d, 2 * n_cb, tn).transpose(1, 0, 2)
    cw3 = conv_w.reshape(CONV_W, 2 * n_cb, tn).transpose(1, 0, 2)
    cb3 = conv_b.reshape(2 * n_cb, 1, tn)
    whole = lambda a: pl.BlockSpec(a.shape, lambda i: (0,) * a.ndim,
                                   pipeline_mode=pl.Buffered(1))
    return pl.pallas_call(
        functools.partial(_ffn_up_kernel, tiles_per_seq=l_pad // ROW_TILE),
        grid=(rows // ROW_TILE,),
        in_specs=[
            pl.BlockSpec((ROW_TILE, d), lambda i: (i, 0)),
            pl.BlockSpec((1, d), lambda i: (0, 0)),
            whole(w3), whole(cw3), whole(cb3),
        ],
        out_specs=pl.BlockSpec((n_cb, ROW_TILE, tn), lambda i: (0, i, 0)),
        out_shape=jax.ShapeDtypeStruct((n_cb, rows, tn), BF16),
        scratch_shapes=[
            pltpu.VMEM((ROW_TILE, d), BF16),
            pltpu.VMEM((2 * n_cb, 8, tn), F32),
        ],
        compiler_params=_params("arbitrary"),
        name="ffn_up",
    )(h, g.reshape(1, d), w3, cw3, cb3)


def _diff_attn_kernel(q_ref, k_ref, vt_ref, lq1_ref, lk1_ref, lq2_ref, lk2_ref, gain_ref,
                      o_ref, qs_ref, m_ref, l_ref, acc_ref, kmax_ref, *, lam_init):
    i = pl.program_id(2)
    tq = q_ref.shape[0]
    tk = tq
    n_t = vt_ref.shape[0]
    sub = ATT_SUB
    per_map = tq // sub
    n_sub = 2 * per_map

    @pl.when(i == 0)
    def _():
        def body(t, mx):
            kb = k_ref[pl.ds(pl.multiple_of(t * tk, tk), tk), :].astype(F32)
            return jnp.maximum(mx, jnp.max(jnp.sum(kb * kb, axis=-1, keepdims=True)))
        kmax_ref[0] = lax.fori_loop(0, n_t, body, jnp.float32(0.0))

    q = q_ref[...]
    lane = lax.broadcasted_iota(jnp.int32, (sub, 2 * DA_DH), 1)
    for c in range(per_map):
        q_c = q[c * sub:(c + 1) * sub, :]
        qs_ref[c] = jnp.where(lane < DA_DH, q_c, jnp.zeros_like(q_c))
        qs_ref[per_map + c] = jnp.where(lane >= DA_DH, q_c, jnp.zeros_like(q_c))
    l_ref[...] = jnp.zeros_like(l_ref)
    acc_ref[...] = jnp.zeros_like(acc_ref)
    qf = q.astype(F32)
    qmax = jnp.max(jnp.sum(qf * qf, axis=-1, keepdims=True))
    no_shift = qmax * kmax_ref[0] <= ATT_NOSHIFT_LIMIT ** 2

    def kv_blocks(blocks, shifted):
        def n_keys(masked, c):
            return ((c % per_map) + 1) * sub if masked else tk

        def scores(j, masked, c):
            row0 = pl.multiple_of(j * tk, tk)
            return lax.dot_general(k_ref[pl.ds(row0, n_keys(masked, c)), :], qs_ref[c],
                                   (((1,), (1,)), ((), ())),
                                   preferred_element_type=F32)

        def softmax(masked, c, s):
            if masked:
                key = lax.broadcasted_iota(jnp.int32, s.shape, 0)
                qry = lax.broadcasted_iota(jnp.int32, s.shape, 1) + (c % per_map) * sub
                s = jnp.where(key <= qry, s, NEG_BIG)
            if not shifted:
                p = jnp.exp2(s)
                l_ref[c] = l_ref[c] + jnp.sum(p, axis=0, keepdims=True)
                return None, p.astype(BF16)
            m_prev = m_ref[c]
            m_new = jnp.maximum(m_prev, jnp.max(s, axis=0, keepdims=True))
            alpha = jnp.exp2(m_prev - m_new)
            p = jnp.exp2(s - m_new)
            l_ref[c] = alpha * l_ref[c] + jnp.sum(p, axis=0, keepdims=True)
            m_ref[c] = m_new
            return alpha, p.astype(BF16)

        stages = [(j, masked, c) for j, masked in blocks for c in range(n_sub)]
        s_q = [scores(*stages[0]), scores(*stages[1])]
        ap = softmax(*stages[0][1:], s_q.pop(0))
        for g, (j, masked, c) in enumerate(stages):
            if g + 2 < len(stages):
                s_q.append(scores(*stages[g + 2]))
            alpha, p = ap
            if g + 1 < len(stages):
                ap = softmax(*stages[g + 1][1:], s_q.pop(0))
            pv = jnp.dot(vt_ref[j, :, pl.ds(0, n_keys(masked, c))], p,
                         preferred_element_type=F32)
            acc_ref[c] = (acc_ref[c] if alpha is None else alpha * acc_ref[c]) + pv

    def all_blocks(shifted, per_iter):
        def body(t, carry):
            kv_blocks([(t * per_iter + u, False) for u in range(per_iter)], shifted)
            return carry
        lax.fori_loop(0, i // per_iter, body, 0)
        for rem in range(per_iter):
            @pl.when(i % per_iter == rem)
            def _():
                kv_blocks([(i - rem + u, False) for u in range(rem)] + [(i, True)], shifted)

    @pl.when(no_shift)
    def _():
        all_blocks(False, ATT_BLOCKS_PER_ITER)

    @pl.when(jnp.logical_not(no_shift))
    def _():
        m_ref[...] = jnp.full_like(m_ref, NEG_BIG)
        all_blocks(True, 1)

    lam = (jnp.exp(jnp.sum(lq1_ref[...] * lk1_ref[...], axis=-1, keepdims=True))
           - jnp.exp(jnp.sum(lq2_ref[...] * lk2_ref[...], axis=-1, keepdims=True))
           + lam_init)
    for c in range(per_map):
        o_t = (acc_ref[c] / l_ref[c]
               - lam * (acc_ref[per_map + c] / l_ref[per_map + c]))
        o = o_t.T
        o = o * lax.rsqrt(jnp.mean(o * o, axis=-1, keepdims=True) + EPS) * gain_ref[...]
        o_ref[c * sub:(c + 1) * sub, :] = (o * (1.0 - lam_init)).astype(o_ref.dtype)


def diff_attention_core(q, k, vt, lq1, lk1, lq2, lk2, sub_gain, lam_init, batch, l_pad):
    rows = q.shape[0]
    t = ROW_TILE
    n_t = l_pad // t
    n_sub = 2 * t // ATT_SUB
    vec = lambda a: a.reshape(1, -1)
    small = lambda n: pl.BlockSpec((1, n), lambda b, h, i: (0, 0))
    return pl.pallas_call(
        functools.partial(_diff_attn_kernel, lam_init=lam_init),
        grid=(batch, DA_HEADS, n_t),
        in_specs=[
            pl.BlockSpec((t, 2 * DA_DH), lambda b, h, i: (b * n_t + i, h)),
            pl.BlockSpec((l_pad, 2 * DA_DH), lambda b, h, i: (b, h)),
            pl.BlockSpec((n_t, DA_DV, t), lambda b, h, i: (b, h, 0)),
            small(DA_DH), small(DA_DH), small(DA_DH), small(DA_DH), small(DA_DV),
        ],
        out_specs=pl.BlockSpec((t, DA_DV), lambda b, h, i: (b * n_t + i, h)),
        out_shape=jax.ShapeDtypeStruct((rows, D_MODEL), BF16),
        scratch_shapes=[
            pltpu.VMEM((n_sub, ATT_SUB, 2 * DA_DH), BF16),
            pltpu.VMEM((n_sub, 1, ATT_SUB), F32),
            pltpu.VMEM((n_sub, 1, ATT_SUB), F32),
            pltpu.VMEM((n_sub, DA_DV, ATT_SUB), F32),
            pltpu.SMEM((1,), F32),
        ],
        compiler_params=_params("arbitrary", "arbitrary", "arbitrary"),
        name="diff_attention",
    )(q, k, vt, vec(lq1), vec(lk1), vec(lq2), vec(lk2), vec(sub_gain))


def kernel(x, meta_tokens, norm_mix, norm_ffn, hg_w_in, hg_lb, hg_gain_o, hg_w_o, kv_norm, w_kv, da_w_q, da_lam_q1, da_lam_k1, da_lam_q2, da_lam_k2, da_sub_gain, da_w_o, ffn_w_up, ffn_conv_w, ffn_conv_b, ffn_w_down, final_norm_g):
    batch, seq, d = x.shape
    n_a = hg_w_in.shape[0]
    depth = norm_mix.shape[0]
    l_real = N_META + seq
    l_pad = -(-l_real // L_ALIGN) * L_ALIGN

    meta = jnp.broadcast_to(meta_tokens[None].astype(x.dtype), (batch, N_META, d))
    tail = jnp.zeros((batch, l_pad - l_real, d), x.dtype)
    h = jnp.concatenate([meta, x, tail], axis=1).reshape(batch * l_pad, d)

    k_sh = vt_sh = None
    for l in range(depth):
        if l < n_a:
            proj = norm_matmul(h, norm_mix[l], hg_w_in[l].astype(BF16), F32)
            o = gla_mixer(proj, hg_lb, hg_gain_o[l], l, batch, l_pad)
            h = matmul_residual(o, hg_w_o[l].astype(BF16), h)
        else:
            j = l - n_a
            lam_init = 0.8 - 0.6 * math.exp(-0.3 * l)
            q = norm_matmul(h, norm_mix[l], da_w_q[j].astype(BF16), BF16,
                            scale=DA_DH ** -0.5 * LOG2E)
            o = diff_attention_core(q, k_sh, vt_sh, da_lam_q1[j], da_lam_k1[j], da_lam_q2[j],
                                    da_lam_k2[j], da_sub_gain[j], lam_init, batch, l_pad)
            h = matmul_residual(o, da_w_o[j].astype(BF16), h)
        gated = ffn_up(h, norm_ffn[l], ffn_w_up[l].astype(BF16), ffn_conv_w[l],
                       ffn_conv_b[l], l_pad)
        w_down = ffn_w_down[l].astype(BF16).reshape(gated.shape[0], gated.shape[2], d)
        h = matmul_residual(gated, w_down, h,
                            norm_g=final_norm_g if l == depth - 1 else None)
        if l == n_a - 1:
            k_sh = norm_matmul(h, kv_norm, w_kv[:, :DA_QK].astype(BF16), BF16)
            vt_sh = norm_matmul_t(h, kv_norm, w_kv[:, DA_QK:].T.astype(BF16), BF16)

    return h.reshape(batch, l_pad, d)[:, N_META:l_real]
```

```python
import functools
import math

import jax
import jax.numpy as jnp
from jax import lax
from jax.experimental import pallas as pl
from jax.experimental.pallas import tpu as pltpu

D_MODEL = 1024
N_META = 16
HG_HEADS = 8
HG_DK = 128
HG_FDIM = HG_HEADS * HG_DK
F_FLOOR = 1e-30
DA_HEADS = 8
DA_DH = 64
DA_DV = 128
DA_QK = DA_HEADS * 2 * DA_DH
D_FF = 2816
CONV_W = 3
EPS = 1e-5

LANES = 128
SUBLANES = 8
MXU_DIM = 256
ROW_TILE = 3 * MXU_DIM
L_ALIGN = ROW_TILE
ATT_SUB = MXU_DIM
LOG2E = 1.4426950408889634
GLA_CHUNK = 64
GLA_SUB = 16
FFN_COL_TILE = 256
PROJ_COL_TILE = 1024
VMEM_LIMIT = 56 * 1024 * 1024
NEG_BIG = -1e30
ATT_NOSHIFT_LIMIT = 48.0
ATT_BLOCKS_PER_ITER = 3
GLA_HEADS_PER_STEP = 8
FFN_ROW_SUB = MXU_DIM
PROJ_ROW_SUB = MXU_DIM

BF16 = jnp.bfloat16
F32 = jnp.float32


def _params(*sem):
    return pltpu.CompilerParams(dimension_semantics=sem, vmem_limit_bytes=VMEM_LIMIT)


def _sigmoid(x):
    return 1.0 / (1.0 + jnp.exp(-x))


def _rms_rows(x, g):
    return x * lax.rsqrt(jnp.mean(x * x, axis=-1, keepdims=True) + EPS) * g


def _norm_matmul_kernel(x_ref, g_ref, w_ref, o_ref, *, scale):
    tm, n = o_ref.shape
    rs = PROJ_ROW_SUB
    tn = min(n, PROJ_COL_TILE)

    def normed(r):
        return _rms_rows(x_ref[r * rs:(r + 1) * rs, :], g_ref[...]).astype(BF16)

    xn_next = normed(0)
    for r in range(tm // rs):
        xn = xn_next
        if (r + 1) * rs < tm:
            xn_next = normed(r + 1)
        for c in range(n // tn):
            y = jnp.dot(xn, w_ref[:, c * tn:(c + 1) * tn], preferred_element_type=F32)
            if scale != 1.0:
                y = y * scale
            o_ref[r * rs:(r + 1) * rs, c * tn:(c + 1) * tn] = y.astype(o_ref.dtype)


def norm_matmul(x, g, w, out_dtype, scale=1.0):
    rows, d = x.shape
    n = w.shape[1]
    return pl.pallas_call(
        functools.partial(_norm_matmul_kernel, scale=scale),
        grid=(rows // ROW_TILE,),
        in_specs=[
            pl.BlockSpec((ROW_TILE, d), lambda i: (i, 0)),
            pl.BlockSpec((1, d), lambda i: (0, 0)),
            pl.BlockSpec((d, n), lambda i: (0, 0), pipeline_mode=pl.Buffered(1)),
        ],
        out_specs=pl.BlockSpec((ROW_TILE, n), lambda i: (i, 0)),
        out_shape=jax.ShapeDtypeStruct((rows, n), out_dtype),
        compiler_params=_params("arbitrary"),
        name="norm_matmul",
    )(x, g.reshape(1, d), w)


def _norm_matmul_t_kernel(x_ref, g_ref, wt_ref, o_ref):
    xn = _rms_rows(x_ref[...], g_ref[...]).astype(BF16)
    o_ref[...] = lax.dot_general(wt_ref[...], xn, (((1,), (1,)), ((), ())),
                                 preferred_element_type=F32).astype(o_ref.dtype)


def norm_matmul_t(x, g, wt, out_dtype):
    rows, d = x.shape
    n = wt.shape[0]
    return pl.pallas_call(
        _norm_matmul_t_kernel,
        grid=(rows // ROW_TILE,),
        in_specs=[
            pl.BlockSpec((ROW_TILE, d), lambda i: (i, 0)),
            pl.BlockSpec((1, d), lambda i: (0, 0)),
            pl.BlockSpec((n, d), lambda i: (0, 0)),
        ],
        out_specs=pl.BlockSpec((None, n, ROW_TILE), lambda i: (i, 0, 0)),
        out_shape=jax.ShapeDtypeStruct((rows // ROW_TILE, n, ROW_TILE), out_dtype),
        compiler_params=_params("arbitrary"),
        name="norm_matmul_t",
    )(x, g.reshape(1, d), wt)


def _blocked_dot(a_ref, w_ref):
    if len(a_ref.shape) == 2:
        return jnp.dot(a_ref[...], w_ref[...], preferred_element_type=F32)
    acc = jnp.dot(a_ref[0], w_ref[0], preferred_element_type=F32)
    for c in range(1, a_ref.shape[0]):
        acc = acc + jnp.dot(a_ref[c], w_ref[c], preferred_element_type=F32)
    return acc


def _matmul_residual_kernel(a_ref, w_ref, h_ref, o_ref):
    o_ref[...] = h_ref[...] + _blocked_dot(a_ref, w_ref)


def _matmul_residual_norm_kernel(a_ref, w_ref, h_ref, g_ref, o_ref):
    o_ref[...] = _rms_rows(h_ref[...] + _blocked_dot(a_ref, w_ref), g_ref[...])


def matmul_residual(a, w, h, norm_g=None):
    n = w.shape[-1]
    rows = a.shape[-2]
    if a.ndim == 2:
        a_spec = pl.BlockSpec((ROW_TILE, a.shape[1]), lambda i: (i, 0))
        w_spec = pl.BlockSpec(w.shape, lambda i: (0, 0), pipeline_mode=pl.Buffered(1))
    else:
        a_spec = pl.BlockSpec((a.shape[0], ROW_TILE, a.shape[2]), lambda i: (0, i, 0))
        w_spec = pl.BlockSpec(w.shape, lambda i: (0, 0, 0), pipeline_mode=pl.Buffered(1))
    in_specs = [a_spec, w_spec, pl.BlockSpec((ROW_TILE, n), lambda i: (i, 0))]
    args = (a, w, h)
    body = _matmul_residual_kernel
    if norm_g is not None:
        in_specs.append(pl.BlockSpec((1, n), lambda i: (0, 0)))
        args += (norm_g.reshape(1, n),)
        body = _matmul_residual_norm_kernel
    return pl.pallas_call(
        body,
        grid=(rows // ROW_TILE,),
        in_specs=in_specs,
        out_specs=pl.BlockSpec((ROW_TILE, n), lambda i: (i, 0)),
        out_shape=jax.ShapeDtypeStruct((rows, n), F32),
        compiler_params=_params("arbitrary"),
        name="matmul_residual",
    )(*args)


def _gla_kernel(q_ref, f_ref, i_ref, g_ref, lb_ref, gain_ref, o_ref,
                st_ref, b_scr, k_scr, v_scr, *, layer, n_chunks):
    C, SUB, HALF = GLA_CHUNK, GLA_SUB, GLA_SUB // 2
    n_sub = C // SUB
    heads = range(GLA_HEADS_PER_STEP)
    lanes = [slice(h * HG_DK, (h + 1) * HG_DK) for h in heads]

    @pl.when(pl.program_id(2) == 0)
    def _():
        st_ref[...] = jnp.zeros_like(st_ref)

    lbp = lb_ref[...]
    e = jnp.exp(lbp - jnp.max(lbp, axis=0, keepdims=True))
    p = e / jnp.sum(e, axis=0, keepdims=True)
    lb_all = jnp.sum(p[:layer + 1], axis=0, keepdims=True) - p[0:1]
    gain_all = gain_ref[...]

    row = lax.broadcasted_iota(jnp.int32, (C, C), 0)
    col = lax.broadcasted_iota(jnp.int32, (C, C), 1)
    tril = (row >= col).astype(BF16)
    half_row = lax.broadcasted_iota(jnp.int32, (HALF, 1), 0)
    sub_col = lax.broadcasted_iota(jnp.int32, (SUB, C), 1)
    nt = (((1,), (1,)), ((), ()))

    def chunk(c, carry):
        r0 = pl.multiple_of(c * C, C)
        rows = pl.ds(r0, C)
        q, k, v16, b = [], [], [], []
        for h in heads:
            lb = lb_all[:, lanes[h]]
            qh = q_ref[rows, lanes[h]]
            q.append(qh * _sigmoid(qh))
            sig = _sigmoid(f_ref[rows, lanes[h]])
            lf = jnp.log(jnp.maximum(lb + (1.0 - lb) * sig, F_FLOOR))
            k.append((1.0 - lb) * (1.0 - sig))
            v = i_ref[rows, lanes[h]]
            v_scr[h] = v
            v16.append(v.astype(BF16))
            lf_hi = lf.astype(BF16)
            rem = lf - lf_hi.astype(F32)
            lf_mid = rem.astype(BF16)
            lf_lo = (rem - lf_mid.astype(F32)).astype(BF16)
            b.append(jnp.dot(tril, lf_hi, preferred_element_type=F32)
                     + jnp.dot(tril, lf_mid, preferred_element_type=F32)
                     + jnp.dot(tril, lf_lo, preferred_element_type=F32))

        st, o_inter, a_off, kv, decay = [], [], [], [], []
        for h in heads:
            b_scr[h] = b[h]
            k_scr[h] = k[h]
            b_last = b_scr[h, C - 1:C, :]
            st.append(st_ref[h])
            qb = (q[h] * jnp.exp(b[h])).astype(BF16)
            o_inter.append(lax.dot_general(qb, st[h].astype(BF16), nt,
                                           preferred_element_type=F32))
            kd = (k[h] * jnp.exp(b_last - b[h])).astype(BF16)
            kv.append(lax.dot_general(v16[h], kd, (((0,), (0,)), ((), ())),
                                      preferred_element_type=F32))
            decay.append(jnp.exp(b_last))
            a_rows = [jnp.zeros((SUB, C), F32)]
            for i in range(1, n_sub):
                lo = i * SUB
                r_i = b_scr[h, lo - 1:lo, :]
                q_s = (q[h][lo:lo + SUB] * jnp.exp(b[h][lo:lo + SUB] - r_i)).astype(BF16)
                k_s = (k[h] * jnp.exp(jnp.minimum(r_i - b[h], 0.0))).astype(BF16)
                a_i = lax.dot_general(q_s, k_s, nt, preferred_element_type=F32)
                a_rows.append(jnp.where(sub_col < lo, a_i, 0.0))
            a_off.append(jnp.concatenate(a_rows, axis=0).astype(BF16))

        o_diag = []
        for h in heads:
            pieces = []
            for i in range(n_sub):
                lo = i * SUB
                q_lo, q_hi = q[h][lo:lo + HALF], q[h][lo + HALF:lo + SUB]
                b_lo, b_hi = b[h][lo:lo + HALF], b[h][lo + HALF:lo + SUB]
                acc_lo = jnp.zeros((HALF, HG_DK), F32)
                acc_hi = jnp.zeros((HALF, HG_DK), F32)
                for s in range(SUB):
                    b_s = b_scr[h, lo + s:lo + s + 1, :]
                    k_row = k_scr[h, lo + s:lo + s + 1, :]
                    v_row = v_scr[h, lo + s:lo + s + 1, :]
                    w_hi = jnp.sum(q_hi * k_row * jnp.exp(b_hi - b_s), axis=-1, keepdims=True)
                    if s < HALF:
                        w_lo = jnp.sum(q_lo * k_row * jnp.exp(b_lo - b_s), axis=-1,
                                       keepdims=True)
                        acc_lo = acc_lo + jnp.where(half_row >= s, w_lo, 0.0) * v_row
                        acc_hi = acc_hi + w_hi * v_row
                    else:
                        acc_hi = acc_hi + jnp.where(half_row >= s - HALF, w_hi, 0.0) * v_row
                pieces += [acc_lo, acc_hi]
            o_diag.append(jnp.concatenate(pieces, axis=0))

        for h in heads:
            o = (o_inter[h] + jnp.dot(a_off[h], v16[h], preferred_element_type=F32)
                 + o_diag[h])
            st_ref[h] = st[h] * decay[h] + kv[h]
            g = g_ref[rows, lanes[h]]
            o = o * lax.rsqrt(jnp.mean(o * o, axis=-1, keepdims=True) + EPS)
            o_ref[rows, lanes[h]] = (o * gain_all[:, lanes[h]] * (g * _sigmoid(g))
                                     ).astype(o_ref.dtype)
        return carry

    lax.fori_loop(0, n_chunks, chunk, 0)


def gla_mixer(proj, hg_lb, gain, layer, batch, l_pad):
    rows = proj.shape[0]
    n_t = l_pad // ROW_TILE
    n_layers = hg_lb.shape[0]
    hps = GLA_HEADS_PER_STEP
    width = hps * HG_DK
    groups = HG_HEADS // hps
    col = lambda part: (lambda b, h, t: (b * n_t + t, part * groups + h))
    return pl.pallas_call(
        functools.partial(_gla_kernel, layer=layer, n_chunks=ROW_TILE // GLA_CHUNK),
        grid=(batch, groups, n_t),
        in_specs=[
            pl.BlockSpec((ROW_TILE, width), col(0)),
            pl.BlockSpec((ROW_TILE, width), col(1)),
            pl.BlockSpec((ROW_TILE, width), col(2)),
            pl.BlockSpec((ROW_TILE, width), col(3)),
            pl.BlockSpec((n_layers, width), lambda b, h, t: (0, h)),
            pl.BlockSpec((1, width), lambda b, h, t: (0, h)),
        ],
        out_specs=pl.BlockSpec((ROW_TILE, width), col(0)),
        out_shape=jax.ShapeDtypeStruct((rows, D_MODEL), BF16),
        scratch_shapes=[
            pltpu.VMEM((hps, HG_DK, HG_DK), F32),
            pltpu.VMEM((hps, GLA_CHUNK, HG_DK), F32),
            pltpu.VMEM((hps, GLA_CHUNK, HG_DK), F32),
            pltpu.VMEM((hps, GLA_CHUNK, HG_DK), F32),
        ],
        compiler_params=_params("arbitrary", "arbitrary", "arbitrary"),
        name="gla_mixer",
    )(proj, proj, proj, proj, hg_lb, gain.reshape(1, D_MODEL))


def _ffn_up_kernel(x_ref, g_ref, w_ref, cw_ref, cb_ref, o_ref, xs_ref, xn_ref, gate_ref,
                   carry_ref, *, tiles_per_seq):
    i = pl.program_id(0)
    tm = xn_ref.shape[0]
    rs = FFN_ROW_SUB
    n_cb = o_ref.shape[0]
    tn = o_ref.shape[-1]
    grp = SUBLANES
    win = grp * grp
    n_win = rs // win

    def interleave(ref, lead, base):
        return jnp.concatenate(
            [jnp.concatenate([ref[lead + (j, pl.ds(base + v, grp, stride=grp), slice(None))]
                              for j in range(ref.shape[len(lead)])], axis=1)
             for v in range(grp)], axis=0)

    for j in range(xs_ref.shape[0]):
        xs_ref[j] = x_ref[:, j * LANES:(j + 1) * LANES]
    for w in range(tm // win):
        xn_ref[w * win:(w + 1) * win, :] = _rms_rows(
            interleave(xs_ref, (), w * win), g_ref[...]).astype(BF16)

    @pl.when((i % tiles_per_seq) == 0)
    def _():
        carry_ref[...] = jnp.zeros_like(carry_ref)

    sub_row = lax.broadcasted_iota(jnp.int32, (grp, 1), 0)

    def project(c, r):
        xr = xn_ref[r * rs:(r + 1) * rs, :]
        shape = (n_win, grp, grp, tn)
        return (jnp.dot(xr, w_ref[c], preferred_element_type=F32).reshape(shape),
                jnp.dot(xr, w_ref[n_cb + c], preferred_element_type=F32).reshape(shape))

    def wrapped(prev_g, cur_g):
        return jnp.where(sub_row == 0, pltpu.roll(prev_g, 1, axis=0),
                         pltpu.roll(cur_g, 1, axis=0))[None]

    def conv(prev, u, cb):
        cw = cw_ref[cb]
        outs = []
        for g in range(n_win):
            cur = u[g]
            before = prev if g == 0 else u[g - 1, grp - 2:]
            w1 = wrapped(before[1], cur[grp - 1])
            w2 = wrapped(before[0], cur[grp - 2])
            u1 = jnp.concatenate([w1, cur[:grp - 1]], axis=0)
            u2 = jnp.concatenate([w2, w1, cur[:grp - 2]], axis=0)
            outs.append((u2 * cw[0:1, :] + u1 * cw[1:2, :] + cur * cw[2:3, :]
                         + cb_ref[cb])[None])
        return jnp.concatenate(outs, axis=0)

    stages = [(c, r) for c in range(n_cb) for r in range(tm // rs)]
    u_next = project(*stages[0])
    prev_a = prev_b = None
    for idx, (c, r) in enumerate(stages):
        ua, ub = u_next
        if idx + 1 < len(stages):
            u_next = project(*stages[idx + 1])
        if r == 0:
            prev_a, prev_b = carry_ref[c], carry_ref[n_cb + c]
        a = conv(prev_a, ua, c)
        b = conv(prev_b, ub, n_cb + c)
        prev_a, prev_b = ua[n_win - 1, grp - 2:], ub[n_win - 1, grp - 2:]
        if (r + 1) * rs == tm:
            carry_ref[c] = prev_a
            carry_ref[n_cb + c] = prev_b
        slot = idx % 2
        gated = (a * _sigmoid(a) * b).reshape(rs, tn)
        for j in range(gate_ref.shape[1]):
            gate_ref[slot, j] = gated[:, j * LANES:(j + 1) * LANES]
        natural = jnp.concatenate(
            [interleave(gate_ref, (slot,), w * win) for w in range(n_win)], axis=0)
        o_ref[c, r * rs:(r + 1) * rs, :] = natural.astype(o_ref.dtype)


def ffn_up(h, g, w_up, conv_w, conv_b, l_pad):
    rows, d = h.shape
    tn = FFN_COL_TILE
    n_cb = D_FF // tn
    w3 = w_up.reshape(d, 2 * n_cb, tn).transpose(1, 0, 2)
    cw3 = conv_w.reshape(CONV_W, 2 * n_cb, tn).transpose(1, 0, 2)
    cb3 = conv_b.reshape(2 * n_cb, 1, tn)
    whole = lambda a: pl.BlockSpec(a.shape, lambda i: (0,) * a.ndim,
                                   pipeline_mode=pl.Buffered(1))
    return pl.pallas_call(
        functools.partial(_ffn_up_kernel, tiles_per_seq=l_pad // ROW_TILE),
        grid=(rows // ROW_TILE,),
        in_specs=[
            pl.BlockSpec((ROW_TILE, d), lambda i: (i, 0)),
            pl.BlockSpec((1, d), lambda i: (0, 0)),
            whole(w3), whole(cw3), whole(cb3),
        ],
        out_specs=pl.BlockSpec((n_cb, ROW_TILE, tn), lambda i: (0, i, 0)),
        out_shape=jax.ShapeDtypeStruct((n_cb, rows, tn), BF16),
        scratch_shapes=[
            pltpu.VMEM((d // LANES, ROW_TILE, LANES), F32),
            pltpu.VMEM((ROW_TILE, d), BF16),
            pltpu.VMEM((2, tn // LANES, FFN_ROW_SUB, LANES), F32),
            pltpu.VMEM((2 * n_cb, 2, SUBLANES, tn), F32),
        ],
        compiler_params=_params("arbitrary"),
        name="ffn_up",
    )(h, g.reshape(1, d), w3, cw3, cb3)


def _diff_attn_kernel(q_ref, k_ref, vt_ref, lq1_ref, lk1_ref, lq2_ref, lk2_ref, gain_ref,
                      o_ref, qs_ref, m_ref, l_ref, acc_ref, kmax_ref, *, lam_init):
    i = pl.program_id(2)
    tq = q_ref.shape[0]
    tk = tq
    n_t = vt_ref.shape[0]
    sub = ATT_SUB
    per_map = tq // sub
    n_sub = 2 * per_map

    @pl.when(i == 0)
    def _():
        def body(t, mx):
            kb = k_ref[pl.ds(pl.multiple_of(t * tk, tk), tk), :].astype(F32)
            return jnp.maximum(mx, jnp.max(jnp.sum(kb * kb, axis=-1, keepdims=True)))
        kmax_ref[0] = lax.fori_loop(0, n_t, body, jnp.float32(0.0))

    q = q_ref[...]
    lane = lax.broadcasted_iota(jnp.int32, (sub, 2 * DA_DH), 1)
    for c in range(per_map):
        q_c = q[c * sub:(c + 1) * sub, :]
        qs_ref[c] = jnp.where(lane < DA_DH, q_c, jnp.zeros_like(q_c))
        qs_ref[per_map + c] = jnp.where(lane >= DA_DH, q_c, jnp.zeros_like(q_c))
    l_ref[...] = jnp.zeros_like(l_ref)
    acc_ref[...] = jnp.zeros_like(acc_ref)
    qf = q.astype(F32)
    qmax = jnp.max(jnp.sum(qf * qf, axis=-1, keepdims=True))
    no_shift = qmax * kmax_ref[0] <= ATT_NOSHIFT_LIMIT ** 2

    def kv_blocks(blocks, shifted):
        def n_keys(masked, c):
            return ((c % per_map) + 1) * sub if masked else tk

        def scores(j, masked, c):
            row0 = pl.multiple_of(j * tk, tk)
            return lax.dot_general(k_ref[pl.ds(row0, n_keys(masked, c)), :], qs_ref[c],
                                   (((1,), (1,)), ((), ())),
                                   preferred_element_type=F32)

        def softmax(masked, c, s):
            if masked:
                key = lax.broadcasted_iota(jnp.int32, s.shape, 0)
                qry = lax.broadcasted_iota(jnp.int32, s.shape, 1) + (c % per_map) * sub
                s = jnp.where(key <= qry, s, NEG_BIG)
            if not shifted:
                p = jnp.exp2(s)
                l_ref[c] = l_ref[c] + jnp.sum(p, axis=0, keepdims=True)
                return None, p.astype(BF16)
            m_prev = m_ref[c]
            m_new = jnp.maximum(m_prev, jnp.max(s, axis=0, keepdims=True))
            alpha = jnp.exp2(m_prev - m_new)
            p = jnp.exp2(s - m_new)
            l_ref[c] = alpha * l_ref[c] + jnp.sum(p, axis=0, keepdims=True)
            m_ref[c] = m_new
            return alpha, p.astype(BF16)

        stages = [(j, masked, c) for j, masked in blocks for c in range(n_sub)]
        s_q = [scores(*stages[0]), scores(*stages[1])]
        ap = softmax(*stages[0][1:], s_q.pop(0))
        for g, (j, masked, c) in enumerate(stages):
            if g + 2 < len(stages):
                s_q.append(scores(*stages[g + 2]))
            alpha, p = ap
            if g + 1 < len(stages):
                ap = softmax(*stages[g + 1][1:], s_q.pop(0))
            pv = jnp.dot(vt_ref[j, :, pl.ds(0, n_keys(masked, c))], p,
                         preferred_element_type=F32)
            acc_ref[c] = (acc_ref[c] if alpha is None else alpha * acc_ref[c]) + pv

    def all_blocks(shifted, per_iter):
        def body(t, carry):
            kv_blocks([(t * per_iter + u, False) for u in range(per_iter)], shifted)
            return carry
        lax.fori_loop(0, i // per_iter, body, 0)
        for rem in range(per_iter):
            @pl.when(i % per_iter == rem)
            def _():
                kv_blocks([(i - rem + u, False) for u in range(rem)] + [(i, True)], shifted)

    @pl.when(no_shift)
    def _():
        all_blocks(False, ATT_BLOCKS_PER_ITER)

    @pl.when(jnp.logical_not(no_shift))
    def _():
        m_ref[...] = jnp.full_like(m_ref, NEG_BIG)
        all_blocks(True, 1)

    lam = (jnp.exp(jnp.sum(lq1_ref[...] * lk1_ref[...], axis=-1, keepdims=True))
           - jnp.exp(jnp.sum(lq2_ref[...] * lk2_ref[...], axis=-1, keepdims=True))
           + lam_init)
    for c in range(per_map):
        o_t = (acc_ref[c] / l_ref[c]
               - lam * (acc_ref[per_map + c] / l_ref[per_map + c]))
        o = o_t.T
        o = o * lax.rsqrt(jnp.mean(o * o, axis=-1, keepdims=True) + EPS) * gain_ref[...]
        o_ref[c * sub:(c + 1) * sub, :] = (o * (1.0 - lam_init)).astype(o_ref.dtype)


def diff_attention_core(q, k, vt, lq1, lk1, lq2, lk2, sub_gain, lam_init, batch, l_pad):
    rows = q.shape[0]
    t = ROW_TILE
    n_t = l_pad // t
    n_sub = 2 * t // ATT_SUB
    vec = lambda a: a.reshape(1, -1)
    small = lambda n: pl.BlockSpec((1, n), lambda b, h, i: (0, 0))
    return pl.pallas_call(
        functools.partial(_diff_attn_kernel, lam_init=lam_init),
        grid=(batch, DA_HEADS, n_t),
        in_specs=[
            pl.BlockSpec((t, 2 * DA_DH), lambda b, h, i: (b * n_t + i, h)),
            pl.BlockSpec((l_pad, 2 * DA_DH), lambda b, h, i: (b, h)),
            pl.BlockSpec((n_t, DA_DV, t), lambda b, h, i: (b, h, 0)),
            small(DA_DH), small(DA_DH), small(DA_DH), small(DA_DH), small(DA_DV),
        ],
        out_specs=pl.BlockSpec((t, DA_DV), lambda b, h, i: (b * n_t + i, h)),
        out_shape=jax.ShapeDtypeStruct((rows, D_MODEL), BF16),
        scratch_shapes=[
            pltpu.VMEM((n_sub, ATT_SUB, 2 * DA_DH), BF16),
            pltpu.VMEM((n_sub, 1, ATT_SUB), F32),
            pltpu.VMEM((n_sub, 1, ATT_SUB), F32),
            pltpu.VMEM((n_sub, DA_DV, ATT_SUB), F32),
            pltpu.SMEM((1,), F32),
        ],
        compiler_params=_params("arbitrary", "arbitrary", "arbitrary"),
        name="diff_attention",
    )(q, k, vt, vec(lq1), vec(lk1), vec(lq2), vec(lk2), vec(sub_gain))


def kernel(x, meta_tokens, norm_mix, norm_ffn, hg_w_in, hg_lb, hg_gain_o, hg_w_o, kv_norm, w_kv, da_w_q, da_lam_q1, da_lam_k1, da_lam_q2, da_lam_k2, da_sub_gain, da_w_o, ffn_w_up, ffn_conv_w, ffn_conv_b, ffn_w_down, final_norm_g):
    batch, seq, d = x.shape
    n_a = hg_w_in.shape[0]
    depth = norm_mix.shape[0]
    l_real = N_META + seq
    l_pad = -(-l_real // L_ALIGN) * L_ALIGN

    meta = jnp.broadcast_to(meta_tokens[None].astype(x.dtype), (batch, N_META, d))
    tail = jnp.zeros((batch, l_pad - l_real, d), x.dtype)
    h = jnp.concatenate([meta, x, tail], axis=1).reshape(batch * l_pad, d)

    k_sh = vt_sh = None
    for l in range(depth):
        if l < n_a:
            proj = norm_matmul(h, norm_mix[l], hg_w_in[l].astype(BF16), F32)
            o = gla_mixer(proj, hg_lb, hg_gain_o[l], l, batch, l_pad)
            h = matmul_residual(o, hg_w_o[l].astype(BF16), h)
        else:
            j = l - n_a
            lam_init = 0.8 - 0.6 * math.exp(-0.3 * l)
            q = norm_matmul(h, norm_mix[l], da_w_q[j].astype(BF16), BF16,
                            scale=DA_DH ** -0.5 * LOG2E)
            o = diff_attention_core(q, k_sh, vt_sh, da_lam_q1[j], da_lam_k1[j], da_lam_q2[j],
                                    da_lam_k2[j], da_sub_gain[j], lam_init, batch, l_pad)
            h = matmul_residual(o, da_w_o[j].astype(BF16), h)
        gated = ffn_up(h, norm_ffn[l], ffn_w_up[l].astype(BF16), ffn_conv_w[l],
                       ffn_conv_b[l], l_pad)
        w_down = ffn_w_down[l].astype(BF16).reshape(gated.shape[0], gated.shape[2], d)
        h = matmul_residual(gated, w_down, h,
                            norm_g=final_norm_g if l == depth - 1 else None)
        if l == n_a - 1:
            k_sh = norm_matmul(h, kv_norm, w_kv[:, :DA_QK].astype(BF16), BF16)
            vt_sh = norm_matmul_t(h, kv_norm, w_kv[:, DA_QK:].T.astype(BF16), BF16)

    return h.reshape(batch, l_pad, d)[:, N_META:l_real]
```

```python
import functools
import math

import jax
import jax.numpy as jnp
from jax import lax
from jax.experimental import pallas as pl
from jax.experimental.pallas import tpu as pltpu

D_MODEL = 1024
N_META = 16
HG_HEADS = 8
HG_DK = 128
HG_FDIM = HG_HEADS * HG_DK
F_FLOOR = 1e-30
DA_HEADS = 8
DA_DH = 64
DA_DV = 128
DA_QK = DA_HEADS * 2 * DA_DH
D_FF = 2816
CONV_W = 3
EPS = 1e-5

LANES = 128
SUBLANES = 8
MXU_DIM = 256
ROW_TILE = 3 * MXU_DIM
L_ALIGN = ROW_TILE
ATT_SUB = MXU_DIM
LOG2E = 1.4426950408889634
GLA_CHUNK = 64
GLA_SUB = 16
FFN_COL_TILE = 256
PROJ_COL_TILE = 1024
VMEM_LIMIT = 56 * 1024 * 1024
NEG_BIG = -1e30
ATT_NOSHIFT_LIMIT = 48.0
ATT_BLOCKS_PER_ITER = 4
GLA_HEADS_PER_STEP = 8
FFN_ROW_SUB = MXU_DIM
PROJ_ROW_SUB = MXU_DIM

BF16 = jnp.bfloat16
F32 = jnp.float32


def _params(*sem):
    return pltpu.CompilerParams(dimension_semantics=sem, vmem_limit_bytes=VMEM_LIMIT)


def _sigmoid(x):
    return 1.0 / (1.0 + jnp.exp(-x))


def _rms_rows(x, g):
    return x * lax.rsqrt(jnp.mean(x * x, axis=-1, keepdims=True) + EPS) * g


def _norm_matmul_kernel(x_ref, g_ref, w_ref, o_ref, *, scale):
    tm, n = o_ref.shape
    rs = PROJ_ROW_SUB
    tn = min(n, PROJ_COL_TILE)

    def normed(r):
        return _rms_rows(x_ref[r * rs:(r + 1) * rs, :], g_ref[...]).astype(BF16)

    xn_next = normed(0)
    for r in range(tm // rs):
        xn = xn_next
        if (r + 1) * rs < tm:
            xn_next = normed(r + 1)
        for c in range(n // tn):
            y = jnp.dot(xn, w_ref[:, c * tn:(c + 1) * tn], preferred_element_type=F32)
            if scale != 1.0:
                y = y * scale
            o_ref[r * rs:(r + 1) * rs, c * tn:(c + 1) * tn] = y.astype(o_ref.dtype)


def norm_matmul(x, g, w, out_dtype, scale=1.0):
    rows, d = x.shape
    n = w.shape[1]
    return pl.pallas_call(
        functools.partial(_norm_matmul_kernel, scale=scale),
        grid=(rows // ROW_TILE,),
        in_specs=[
            pl.BlockSpec((ROW_TILE, d), lambda i: (i, 0)),
            pl.BlockSpec((1, d), lambda i: (0, 0)),
            pl.BlockSpec((d, n), lambda i: (0, 0), pipeline_mode=pl.Buffered(1)),
        ],
        out_specs=pl.BlockSpec((ROW_TILE, n), lambda i: (i, 0)),
        out_shape=jax.ShapeDtypeStruct((rows, n), out_dtype),
        compiler_params=_params("arbitrary"),
        name="norm_matmul",
    )(x, g.reshape(1, d), w)


def _norm_matmul_t_kernel(x_ref, g_ref, wt_ref, o_ref):
    xn = _rms_rows(x_ref[...], g_ref[...]).astype(BF16)
    o_ref[...] = lax.dot_general(wt_ref[...], xn, (((1,), (1,)), ((), ())),
                                 preferred_element_type=F32).astype(o_ref.dtype)


def norm_matmul_t(x, g, wt, out_dtype):
    rows, d = x.shape
    n = wt.shape[0]
    return pl.pallas_call(
        _norm_matmul_t_kernel,
        grid=(rows // ROW_TILE,),
        in_specs=[
            pl.BlockSpec((ROW_TILE, d), lambda i: (i, 0)),
            pl.BlockSpec((1, d), lambda i: (0, 0)),
            pl.BlockSpec((n, d), lambda i: (0, 0)),
        ],
        out_specs=pl.BlockSpec((None, n, ROW_TILE), lambda i: (i, 0, 0)),
        out_shape=jax.ShapeDtypeStruct((rows // ROW_TILE, n, ROW_TILE), out_dtype),
        compiler_params=_params("arbitrary"),
        name="norm_matmul_t",
    )(x, g.reshape(1, d), wt)


def _blocked_dot(a_ref, w_ref):
    if len(a_ref.shape) == 2:
        return jnp.dot(a_ref[...], w_ref[...], preferred_element_type=F32)
    acc = jnp.dot(a_ref[0], w_ref[0], preferred_element_type=F32)
    for c in range(1, a_ref.shape[0]):
        acc = acc + jnp.dot(a_ref[c], w_ref[c], preferred_element_type=F32)
    return acc


def _matmul_residual_kernel(a_ref, w_ref, h_ref, o_ref):
    o_ref[...] = h_ref[...] + _blocked_dot(a_ref, w_ref)


def _matmul_residual_norm_kernel(a_ref, w_ref, h_ref, g_ref, o_ref):
    o_ref[...] = _rms_rows(h_ref[...] + _blocked_dot(a_ref, w_ref), g_ref[...])


def matmul_residual(a, w, h, norm_g=None):
    n = w.shape[-1]
    rows = a.shape[-2]
    if a.ndim == 2:
        a_spec = pl.BlockSpec((ROW_TILE, a.shape[1]), lambda i: (i, 0))
        w_spec = pl.BlockSpec(w.shape, lambda i: (0, 0), pipeline_mode=pl.Buffered(1))
    else:
        a_spec = pl.BlockSpec((a.shape[0], ROW_TILE, a.shape[2]), lambda i: (0, i, 0))
        w_spec = pl.BlockSpec(w.shape, lambda i: (0, 0, 0), pipeline_mode=pl.Buffered(1))
    in_specs = [a_spec, w_spec, pl.BlockSpec((ROW_TILE, n), lambda i: (i, 0))]
    args = (a, w, h)
    body = _matmul_residual_kernel
    if norm_g is not None:
        in_specs.append(pl.BlockSpec((1, n), lambda i: (0, 0)))
        args += (norm_g.reshape(1, n),)
        body = _matmul_residual_norm_kernel
    return pl.pallas_call(
        body,
        grid=(rows // ROW_TILE,),
        in_specs=in_specs,
        out_specs=pl.BlockSpec((ROW_TILE, n), lambda i: (i, 0)),
        out_shape=jax.ShapeDtypeStruct((rows, n), F32),
        compiler_params=_params("arbitrary"),
        name="matmul_residual",
    )(*args)


def _gla_kernel(q_ref, f_ref, i_ref, g_ref, lb_ref, gain_ref, o_ref,
                st_ref, b_scr, k_scr, v_scr, *, layer, n_chunks):
    C, SUB, HALF = GLA_CHUNK, GLA_SUB, GLA_SUB // 2
    n_sub = C // SUB
    heads = range(GLA_HEADS_PER_STEP)
    lanes = [slice(h * HG_DK, (h + 1) * HG_DK) for h in heads]

    @pl.when(pl.program_id(2) == 0)
    def _():
        st_ref[...] = jnp.zeros_like(st_ref)

    lbp = lb_ref[...]
    e = jnp.exp(lbp - jnp.max(lbp, axis=0, keepdims=True))
    p = e / jnp.sum(e, axis=0, keepdims=True)
    lb_all = jnp.sum(p[:layer + 1], axis=0, keepdims=True) - p[0:1]
    gain_all = gain_ref[...]

    row = lax.broadcasted_iota(jnp.int32, (C, C), 0)
    col = lax.broadcasted_iota(jnp.int32, (C, C), 1)
    tril = (row >= col).astype(BF16)
    half_row = lax.broadcasted_iota(jnp.int32, (HALF, 1), 0)
    sub_col = lax.broadcasted_iota(jnp.int32, (SUB, C), 1)
    nt = (((1,), (1,)), ((), ()))

    def chunk(c, carry):
        r0 = pl.multiple_of(c * C, C)
        rows = pl.ds(r0, C)
        q, k, v16, b = [], [], [], []
        for h in heads:
            lb = lb_all[:, lanes[h]]
            qh = q_ref[rows, lanes[h]]
            q.append(qh * _sigmoid(qh))
            sig = _sigmoid(f_ref[rows, lanes[h]])
            lf = jnp.log(jnp.maximum(lb + (1.0 - lb) * sig, F_FLOOR))
            k.append((1.0 - lb) * (1.0 - sig))
            v = i_ref[rows, lanes[h]]
            v_scr[h] = v
            v16.append(v.astype(BF16))
            lf_hi = lf.astype(BF16)
            rem = lf - lf_hi.astype(F32)
            lf_mid = rem.astype(BF16)
            lf_lo = (rem - lf_mid.astype(F32)).astype(BF16)
            b.append(jnp.dot(tril, lf_hi, preferred_element_type=F32)
                     + jnp.dot(tril, lf_mid, preferred_element_type=F32)
                     + jnp.dot(tril, lf_lo, preferred_element_type=F32))

        st, o_inter, a_off, kv, decay = [], [], [], [], []
        for h in heads:
            b_scr[h] = b[h]
            k_scr[h] = k[h]
            b_last = b_scr[h, C - 1:C, :]
            st.append(st_ref[h])
            qb = (q[h] * jnp.exp(b[h])).astype(BF16)
            o_inter.append(lax.dot_general(qb, st[h].astype(BF16), nt,
                                           preferred_element_type=F32))
            kd = (k[h] * jnp.exp(b_last - b[h])).astype(BF16)
            kv.append(lax.dot_general(v16[h], kd, (((0,), (0,)), ((), ())),
                                      preferred_element_type=F32))
            decay.append(jnp.exp(b_last))
            a_rows = [jnp.zeros((SUB, C), F32)]
            for i in range(1, n_sub):
                lo = i * SUB
                r_i = b_scr[h, lo - 1:lo, :]
                q_s = (q[h][lo:lo + SUB] * jnp.exp(b[h][lo:lo + SUB] - r_i)).astype(BF16)
                k_s = (k[h] * jnp.exp(jnp.minimum(r_i - b[h], 0.0))).astype(BF16)
                a_i = lax.dot_general(q_s, k_s, nt, preferred_element_type=F32)
                a_rows.append(jnp.where(sub_col < lo, a_i, 0.0))
            a_off.append(jnp.concatenate(a_rows, axis=0).astype(BF16))

        o_diag = []
        for h in heads:
            pieces = []
            for i in range(n_sub):
                lo = i * SUB
                q_lo, q_hi = q[h][lo:lo + HALF], q[h][lo + HALF:lo + SUB]
                b_lo, b_hi = b[h][lo:lo + HALF], b[h][lo + HALF:lo + SUB]
                acc_lo = jnp.zeros((HALF, HG_DK), F32)
                acc_hi = jnp.zeros((HALF, HG_DK), F32)
                for s in range(SUB):
                    b_s = b_scr[h, lo + s:lo + s + 1, :]
                    k_row = k_scr[h, lo + s:lo + s + 1, :]
                    v_row = v_scr[h, lo + s:lo + s + 1, :]
                    w_hi = jnp.sum(q_hi * k_row * jnp.exp(b_hi - b_s), axis=-1, keepdims=True)
                    if s < HALF:
                        w_lo = jnp.sum(q_lo * k_row * jnp.exp(b_lo - b_s), axis=-1,
                                       keepdims=True)
                        acc_lo = acc_lo + jnp.where(half_row >= s, w_lo, 0.0) * v_row
                        acc_hi = acc_hi + w_hi * v_row
                    else:
                        acc_hi = acc_hi + jnp.where(half_row >= s - HALF, w_hi, 0.0) * v_row
                pieces += [acc_lo, acc_hi]
            o_diag.append(jnp.concatenate(pieces, axis=0))

        for h in heads:
            o = (o_inter[h] + jnp.dot(a_off[h], v16[h], preferred_element_type=F32)
                 + o_diag[h])
            st_ref[h] = st[h] * decay[h] + kv[h]
            g = g_ref[rows, lanes[h]]
            o = o * lax.rsqrt(jnp.mean(o * o, axis=-1, keepdims=True) + EPS)
            o_ref[rows, lanes[h]] = (o * gain_all[:, lanes[h]] * (g * _sigmoid(g))
                                     ).astype(o_ref.dtype)
        return carry

    lax.fori_loop(0, n_chunks, chunk, 0)


def gla_mixer(proj, hg_lb, gain, layer, batch, l_pad):
    rows = proj.shape[0]
    n_t = l_pad // ROW_TILE
    n_layers = hg_lb.shape[0]
    hps = GLA_HEADS_PER_STEP
    width = hps * HG_DK
    groups = HG_HEADS // hps
    col = lambda part: (lambda b, h, t: (b * n_t + t, part * groups + h))
    return pl.pallas_call(
        functools.partial(_gla_kernel, layer=layer, n_chunks=ROW_TILE // GLA_CHUNK),
        grid=(batch, groups, n_t),
        in_specs=[
            pl.BlockSpec((ROW_TILE, width), col(0)),
            pl.BlockSpec((ROW_TILE, width), col(1)),
            pl.BlockSpec((ROW_TILE, width), col(2)),
            pl.BlockSpec((ROW_TILE, width), col(3)),
            pl.BlockSpec((n_layers, width), lambda b, h, t: (0, h)),
            pl.BlockSpec((1, width), lambda b, h, t: (0, h)),
        ],
        out_specs=pl.BlockSpec((ROW_TILE, width), col(0)),
        out_shape=jax.ShapeDtypeStruct((rows, D_MODEL), BF16),
        scratch_shapes=[
            pltpu.VMEM((hps, HG_DK, HG_DK), F32),
            pltpu.VMEM((hps, GLA_CHUNK, HG_DK), F32),
            pltpu.VMEM((hps, GLA_CHUNK, HG_DK), F32),
            pltpu.VMEM((hps, GLA_CHUNK, HG_DK), F32),
        ],
        compiler_params=_params("arbitrary", "arbitrary", "arbitrary"),
        name="gla_mixer",
    )(proj, proj, proj, proj, hg_lb, gain.reshape(1, D_MODEL))


def _ffn_up_kernel(x_ref, g_ref, w_ref, cw_ref, cb_ref, o_ref, xs_ref, xn_ref, gate_ref,
                   carry_ref, *, tiles_per_seq):
    i = pl.program_id(0)
    tm = xn_ref.shape[0]
    rs = FFN_ROW_SUB
    n_cb = o_ref.shape[0]
    tn = o_ref.shape[-1]
    grp = SUBLANES
    win = grp * grp
    n_win = rs // win

    def interleave(ref, lead, base):
        return jnp.concatenate(
            [jnp.concatenate([ref[lead + (j, pl.ds(base + v, grp, stride=grp), slice(None))]
                              for j in range(ref.shape[len(lead)])], axis=1)
             for v in range(grp)], axis=0)

    for j in range(xs_ref.shape[0]):
        xs_ref[j] = x_ref[:, j * LANES:(j + 1) * LANES]

    def normalize_block(r):
        for w in range(r * n_win, (r + 1) * n_win):
            xn_ref[w * win:(w + 1) * win, :] = _rms_rows(
                interleave(xs_ref, (), w * win), g_ref[...]).astype(BF16)

    @pl.when((i % tiles_per_seq) == 0)
    def _():
        carry_ref[...] = jnp.zeros_like(carry_ref)

    sub_row = lax.broadcasted_iota(jnp.int32, (grp, 1), 0)

    def project(c, r):
        xr = xn_ref[r * rs:(r + 1) * rs, :]
        shape = (n_win, grp, grp, tn)
        return (jnp.dot(xr, w_ref[c], preferred_element_type=F32).reshape(shape),
                jnp.dot(xr, w_ref[n_cb + c], preferred_element_type=F32).reshape(shape))

    def wrapped(prev_g, cur_g):
        return jnp.where(sub_row == 0, pltpu.roll(prev_g, 1, axis=0),
                         pltpu.roll(cur_g, 1, axis=0))[None]

    def conv(prev, u, cb):
        cw = cw_ref[cb]
        outs = []
        for g in range(n_win):
            cur = u[g]
            before = prev if g == 0 else u[g - 1, grp - 2:]
            w1 = wrapped(before[1], cur[grp - 1])
            w2 = wrapped(before[0], cur[grp - 2])
            u1 = jnp.concatenate([w1, cur[:grp - 1]], axis=0)
            u2 = jnp.concatenate([w2, w1, cur[:grp - 2]], axis=0)
            outs.append((u2 * cw[0:1, :] + u1 * cw[1:2, :] + cur * cw[2:3, :]
                         + cb_ref[cb])[None])
        return jnp.concatenate(outs, axis=0)

    stages = [(c, r) for c in range(n_cb) for r in range(tm // rs)]
    normalize_block(0)
    u_next = project(*stages[0])
    prev_a = prev_b = None
    for idx, (c, r) in enumerate(stages):
        ua, ub = u_next
        if idx + 1 < len(stages):
            if stages[idx + 1][0] == 0:
                normalize_block(stages[idx + 1][1])
            u_next = project(*stages[idx + 1])
        if r == 0:
            prev_a, prev_b = carry_ref[c], carry_ref[n_cb + c]
        a = conv(prev_a, ua, c)
        b = conv(prev_b, ub, n_cb + c)
        prev_a, prev_b = ua[n_win - 1, grp - 2:], ub[n_win - 1, grp - 2:]
        if (r + 1) * rs == tm:
            carry_ref[c] = prev_a
            carry_ref[n_cb + c] = prev_b
        slot = idx % 2
        gated = (a * _sigmoid(a) * b).reshape(rs, tn)
        for j in range(gate_ref.shape[1]):
            gate_ref[slot, j] = gated[:, j * LANES:(j + 1) * LANES]
        natural = jnp.concatenate(
            [interleave(gate_ref, (slot,), w * win) for w in range(n_win)], axis=0)
        o_ref[c, r * rs:(r + 1) * rs, :] = natural.astype(o_ref.dtype)


def ffn_up(h, g, w_up, conv_w, conv_b, l_pad):
    rows, d = h.shape
    tn = FFN_COL_TILE
    n_cb = D_FF // tn
    w3 = w_up.reshape(d, 2 * n_cb, tn).transpose(1, 0, 2)
    cw3 = conv_w.reshape(CONV_W, 2 * n_cb, tn).transpose(1, 0, 2)
    cb3 = conv_b.reshape(2 * n_cb, 1, tn)
    whole = lambda a: pl.BlockSpec(a.shape, lambda i: (0,) * a.ndim,
                                   pipeline_mode=pl.Buffered(1))
    return pl.pallas_call(
        functools.partial(_ffn_up_kernel, tiles_per_seq=l_pad // ROW_TILE),
        grid=(rows // ROW_TILE,),
        in_specs=[
            pl.BlockSpec((ROW_TILE, d), lambda i: (i, 0)),
            pl.BlockSpec((1, d), lambda i: (0, 0)),
            whole(w3), whole(cw3), whole(cb3),
        ],
        out_specs=pl.BlockSpec((n_cb, ROW_TILE, tn), lambda i: (0, i, 0)),
        out_shape=jax.ShapeDtypeStruct((n_cb, rows, tn), BF16),
        scratch_shapes=[
            pltpu.VMEM((d // LANES, ROW_TILE, LANES), F32),
            pltpu.VMEM((ROW_TILE, d), BF16),
            pltpu.VMEM((2, tn // LANES, FFN_ROW_SUB, LANES), F32),
            pltpu.VMEM((2 * n_cb, 2, SUBLANES, tn), F32),
        ],
        compiler_params=_params("arbitrary"),
        name="ffn_up",
    )(h, g.reshape(1, d), w3, cw3, cb3)


def _diff_attn_kernel(q_ref, k_ref, vt_ref, lq1_ref, lk1_ref, lq2_ref, lk2_ref, gain_ref,
                      o_ref, qs_ref, m_ref, l_ref, acc_ref, kmax_ref, *, lam_init):
    i = pl.program_id(2)
    tq = q_ref.shape[0]
    tk = tq
    n_t = vt_ref.shape[0]
    sub = ATT_SUB
    per_map = tq // sub
    n_sub = 2 * per_map

    @pl.when(i == 0)
    def _():
        def body(t, mx):
            kb = k_ref[pl.ds(pl.multiple_of(t * tk, tk), tk), :].astype(F32)
            return jnp.maximum(mx, jnp.max(jnp.sum(kb * kb, axis=-1, keepdims=True)))
        kmax_ref[0] = lax.fori_loop(0, n_t, body, jnp.float32(0.0))

    q = q_ref[...]
    lane = lax.broadcasted_iota(jnp.int32, (sub, 2 * DA_DH), 1)
    for c in range(per_map):
        q_c = q[c * sub:(c + 1) * sub, :]
        qs_ref[c] = jnp.where(lane < DA_DH, q_c, jnp.zeros_like(q_c))
        qs_ref[per_map + c] = jnp.where(lane >= DA_DH, q_c, jnp.zeros_like(q_c))
    l_ref[...] = jnp.zeros_like(l_ref)
    acc_ref[...] = jnp.zeros_like(acc_ref)
    qf = q.astype(F32)
    qmax = jnp.max(jnp.sum(qf * qf, axis=-1, keepdims=True))
    no_shift = qmax * kmax_ref[0] <= ATT_NOSHIFT_LIMIT ** 2

    def kv_blocks(blocks, shifted):
        def n_keys(masked, c):
            return ((c % per_map) + 1) * sub if masked else tk

        def scores(j, masked, c):
            row0 = pl.multiple_of(j * tk, tk)
            return lax.dot_general(k_ref[pl.ds(row0, n_keys(masked, c)), :], qs_ref[c],
                                   (((1,), (1,)), ((), ())),
                                   preferred_element_type=F32)

        def softmax(masked, c, s):
            if masked:
                key = lax.broadcasted_iota(jnp.int32, s.shape, 0)
                qry = lax.broadcasted_iota(jnp.int32, s.shape, 1) + (c % per_map) * sub
                s = jnp.where(key <= qry, s, NEG_BIG)
            if not shifted:
                p = jnp.exp2(s)
                l_ref[c] = l_ref[c] + jnp.sum(p, axis=0, keepdims=True)
                return None, p.astype(BF16)
            m_prev = m_ref[c]
            m_new = jnp.maximum(m_prev, jnp.max(s, axis=0, keepdims=True))
            alpha = jnp.exp2(m_prev - m_new)
            p = jnp.exp2(s - m_new)
            l_ref[c] = alpha * l_ref[c] + jnp.sum(p, axis=0, keepdims=True)
            m_ref[c] = m_new
            return alpha, p.astype(BF16)

        stages = [(j, masked, c) for j, masked in blocks for c in range(n_sub)]
        s_q = [scores(*stages[0]), scores(*stages[1])]
        ap = softmax(*stages[0][1:], s_q.pop(0))
        for g, (j, masked, c) in enumerate(stages):
            if g + 2 < len(stages):
                s_q.append(scores(*stages[g + 2]))
            alpha, p = ap
            if g + 1 < len(stages):
                ap = softmax(*stages[g + 1][1:], s_q.pop(0))
            pv = jnp.dot(vt_ref[j, :, pl.ds(0, n_keys(masked, c))], p,
                         preferred_element_type=F32)
            acc_ref[c] = (acc_ref[c] if alpha is None else alpha * acc_ref[c]) + pv

    def all_blocks(shifted, per_iter):
        def body(t, carry):
            kv_blocks([(t * per_iter + u, False) for u in range(per_iter)], shifted)
            return carry
        lax.fori_loop(0, i // per_iter, body, 0)
        for rem in range(per_iter):
            @pl.when(i % per_iter == rem)
            def _():
                kv_blocks([(i - rem + u, False) for u in range(rem)] + [(i, True)], shifted)

    @pl.when(no_shift)
    def _():
        all_blocks(False, ATT_BLOCKS_PER_ITER)

    @pl.when(jnp.logical_not(no_shift))
    def _():
        m_ref[...] = jnp.full_like(m_ref, NEG_BIG)
        all_blocks(True, 1)

    lam = (jnp.exp(jnp.sum(lq1_ref[...] * lk1_ref[...], axis=-1, keepdims=True))
           - jnp.exp(jnp.sum(lq2_ref[...] * lk2_ref[...], axis=-1, keepdims=True))
           + lam_init)
    for c in range(per_map):
        o_t = (acc_ref[c] / l_ref[c]
               - lam * (acc_ref[per_map + c] / l_ref[per_map + c]))
        o = o_t.T
        o = o * lax.rsqrt(jnp.mean(o * o, axis=-1, keepdims=True) + EPS) * gain_ref[...]
        o_ref[c * sub:(c + 1) * sub, :] = (o * (1.0 - lam_init)).astype(o_ref.dtype)


def diff_attention_core(q, k, vt, lq1, lk1, lq2, lk2, sub_gain, lam_init, batch, l_pad):
    rows = q.shape[0]
    t = ROW_TILE
    n_t = l_pad // t
    n_sub = 2 * t // ATT_SUB
    vec = lambda a: a.reshape(1, -1)
    small = lambda n: pl.BlockSpec((1, n), lambda b, h, i: (0, 0))
    return pl.pallas_call(
        functools.partial(_diff_attn_kernel, lam_init=lam_init),
        grid=(batch, DA_HEADS, n_t),
        in_specs=[
            pl.BlockSpec((t, 2 * DA_DH), lambda b, h, i: (b * n_t + i, h)),
            pl.BlockSpec((l_pad, 2 * DA_DH), lambda b, h, i: (b, h)),
            pl.BlockSpec((n_t, DA_DV, t), lambda b, h, i: (b, h, 0)),
            small(DA_DH), small(DA_DH), small(DA_DH), small(DA_DH), small(DA_DV),
        ],
        out_specs=pl.BlockSpec((t, DA_DV), lambda b, h, i: (b * n_t + i, h)),
        out_shape=jax.ShapeDtypeStruct((rows, D_MODEL), BF16),
        scratch_shapes=[
            pltpu.VMEM((n_sub, ATT_SUB, 2 * DA_DH), BF16),
            pltpu.VMEM((n_sub, 1, ATT_SUB), F32),
            pltpu.VMEM((n_sub, 1, ATT_SUB), F32),
            pltpu.VMEM((n_sub, DA_DV, ATT_SUB), F32),
            pltpu.SMEM((1,), F32),
        ],
        compiler_params=_params("arbitrary", "arbitrary", "arbitrary"),
        name="diff_attention",
    )(q, k, vt, vec(lq1), vec(lk1), vec(lq2), vec(lk2), vec(sub_gain))


def kernel(x, meta_tokens, norm_mix, norm_ffn, hg_w_in, hg_lb, hg_gain_o, hg_w_o, kv_norm, w_kv, da_w_q, da_lam_q1, da_lam_k1, da_lam_q2, da_lam_k2, da_sub_gain, da_w_o, ffn_w_up, ffn_conv_w, ffn_conv_b, ffn_w_down, final_norm_g):
    batch, seq, d = x.shape
    n_a = hg_w_in.shape[0]
    depth = norm_mix.shape[0]
    l_real = N_META + seq
    l_pad = -(-l_real // L_ALIGN) * L_ALIGN

    meta = jnp.broadcast_to(meta_tokens[None].astype(x.dtype), (batch, N_META, d))
    tail = jnp.zeros((batch, l_pad - l_real, d), x.dtype)
    h = jnp.concatenate([meta, x, tail], axis=1).reshape(batch * l_pad, d)

    k_sh = vt_sh = None
    for l in range(depth):
        if l < n_a:
            proj = norm_matmul(h, norm_mix[l], hg_w_in[l].astype(BF16), F32)
            o = gla_mixer(proj, hg_lb, hg_gain_o[l], l, batch, l_pad)
            h = matmul_residual(o, hg_w_o[l].astype(BF16), h)
        else:
            j = l - n_a
            lam_init = 0.8 - 0.6 * math.exp(-0.3 * l)
            q = norm_matmul(h, norm_mix[l], da_w_q[j].astype(BF16), BF16,
                            scale=DA_DH ** -0.5 * LOG2E)
            o = diff_attention_core(q, k_sh, vt_sh, da_lam_q1[j], da_lam_k1[j], da_lam_q2[j],
                                    da_lam_k2[j], da_sub_gain[j], lam_init, batch, l_pad)
            h = matmul_residual(o, da_w_o[j].astype(BF16), h)
        gated = ffn_up(h, norm_ffn[l], ffn_w_up[l].astype(BF16), ffn_conv_w[l],
                       ffn_conv_b[l], l_pad)
        w_down = ffn_w_down[l].astype(BF16).reshape(gated.shape[0], gated.shape[2], d)
        h = matmul_residual(gated, w_down, h,
                            norm_g=final_norm_g if l == depth - 1 else None)
        if l == n_a - 1:
            k_sh = norm_matmul(h, kv_norm, w_kv[:, :DA_QK].astype(BF16), BF16)
            vt_sh = norm_matmul_t(h, kv_norm, w_kv[:, DA_QK:].T.astype(BF16), BF16)

    return h.reshape(batch, l_pad, d)[:, N_META:l_real]
```

```python
import functools
import math

import jax
import jax.numpy as jnp
from jax import lax
from jax.experimental import pallas as pl
from jax.experimental.pallas import tpu as pltpu

D_MODEL = 1024
N_META = 16
HG_HEADS = 8
HG_DK = 128
HG_FDIM = HG_HEADS * HG_DK
F_FLOOR = 1e-30
DA_HEADS = 8
DA_DH = 64
DA_DV = 128
DA_QK = DA_HEADS * 2 * DA_DH
D_FF = 2816
CONV_W = 3
EPS = 1e-5

LANES = 128
SUBLANES = 8
MXU_DIM = 256
ROW_TILE = 3 * MXU_DIM
L_ALIGN = ROW_TILE
ATT_SUB = MXU_DIM
LOG2E = 1.4426950408889634
GLA_CHUNK = 64
GLA_SUB = 16
FFN_COL_TILE = 256
PROJ_COL_TILE = 1024
VMEM_LIMIT = 56 * 1024 * 1024
NEG_BIG = -1e30
ATT_NOSHIFT_LIMIT = 48.0
ATT_BLOCKS_PER_ITER = 4
GLA_HEADS_PER_STEP = 8
FFN_ROW_SUB = MXU_DIM
PROJ_ROW_SUB = MXU_DIM

BF16 = jnp.bfloat16
F32 = jnp.float32


def _params(*sem):
    return pltpu.CompilerParams(dimension_semantics=sem, vmem_limit_bytes=VMEM_LIMIT)


def _sigmoid(x):
    return 1.0 / (1.0 + jnp.exp(-x))


def _rms_rows(x, g):
    return x * lax.rsqrt(jnp.mean(x * x, axis=-1, keepdims=True) + EPS) * g


def _norm_matmul_kernel(x_ref, g_ref, w_ref, o_ref, *, scale):
    tm, n = o_ref.shape
    rs = PROJ_ROW_SUB
    tn = min(n, PROJ_COL_TILE)

    def normed(r):
        return _rms_rows(x_ref[r * rs:(r + 1) * rs, :], g_ref[...]).astype(BF16)

    xn_next = normed(0)
    for r in range(tm // rs):
        xn = xn_next
        if (r + 1) * rs < tm:
            xn_next = normed(r + 1)
        for c in range(n // tn):
            y = jnp.dot(xn, w_ref[:, c * tn:(c + 1) * tn], preferred_element_type=F32)
            if scale != 1.0:
                y = y * scale
            o_ref[r * rs:(r + 1) * rs, c * tn:(c + 1) * tn] = y.astype(o_ref.dtype)


def norm_matmul(x, g, w, out_dtype, scale=1.0):
    rows, d = x.shape
    n = w.shape[1]
    return pl.pallas_call(
        functools.partial(_norm_matmul_kernel, scale=scale),
        grid=(rows // ROW_TILE,),
        in_specs=[
            pl.BlockSpec((ROW_TILE, d), lambda i: (i, 0)),
            pl.BlockSpec((1, d), lambda i: (0, 0)),
            pl.BlockSpec((d, n), lambda i: (0, 0), pipeline_mode=pl.Buffered(1)),
        ],
        out_specs=pl.BlockSpec((ROW_TILE, n), lambda i: (i, 0)),
        out_shape=jax.ShapeDtypeStruct((rows, n), out_dtype),
        compiler_params=_params("arbitrary"),
        name="norm_matmul",
    )(x, g.reshape(1, d), w)


def _shared_kv_kernel(x_ref, g_ref, wk_ref, wvt_ref, k_ref, vt_ref):
    tm = x_ref.shape[0]
    rs = PROJ_ROW_SUB
    for r in range(tm // rs):
        xn = _rms_rows(x_ref[r * rs:(r + 1) * rs, :], g_ref[...]).astype(BF16)
        k_ref[r * rs:(r + 1) * rs, :] = jnp.dot(
            xn, wk_ref[...], preferred_element_type=F32).astype(k_ref.dtype)
        vt_ref[:, r * rs:(r + 1) * rs] = lax.dot_general(
            wvt_ref[...], xn, (((1,), (1,)), ((), ())),
            preferred_element_type=F32).astype(vt_ref.dtype)


def shared_kv_proj(x, g, wk, wvt, out_dtype):
    rows, d = x.shape
    n = wk.shape[1]
    whole = lambda a: pl.BlockSpec(a.shape, lambda i: (0, 0), pipeline_mode=pl.Buffered(1))
    return pl.pallas_call(
        _shared_kv_kernel,
        grid=(rows // ROW_TILE,),
        in_specs=[
            pl.BlockSpec((ROW_TILE, d), lambda i: (i, 0)),
            pl.BlockSpec((1, d), lambda i: (0, 0)),
            whole(wk), whole(wvt),
        ],
        out_specs=[pl.BlockSpec((ROW_TILE, n), lambda i: (i, 0)),
                   pl.BlockSpec((None, n, ROW_TILE), lambda i: (i, 0, 0))],
        out_shape=[jax.ShapeDtypeStruct((rows, n), out_dtype),
                   jax.ShapeDtypeStruct((rows // ROW_TILE, n, ROW_TILE), out_dtype)],
        compiler_params=_params("arbitrary"),
        name="shared_kv_proj",
    )(x, g.reshape(1, d), wk, wvt)


def _blocked_dot(a_ref, w_ref):
    if len(a_ref.shape) == 2:
        return jnp.dot(a_ref[...], w_ref[...], preferred_element_type=F32)
    acc = jnp.dot(a_ref[0], w_ref[0], preferred_element_type=F32)
    for c in range(1, a_ref.shape[0]):
        acc = acc + jnp.dot(a_ref[c], w_ref[c], preferred_element_type=F32)
    return acc


def _matmul_residual_kernel(a_ref, w_ref, h_ref, o_ref):
    o_ref[...] = h_ref[...] + _blocked_dot(a_ref, w_ref)


def _matmul_residual_norm_kernel(a_ref, w_ref, h_ref, g_ref, o_ref):
    o_ref[...] = _rms_rows(h_ref[...] + _blocked_dot(a_ref, w_ref), g_ref[...])


def matmul_residual(a, w, h, norm_g=None):
    n = w.shape[-1]
    rows = a.shape[-2]
    if a.ndim == 2:
        a_spec = pl.BlockSpec((ROW_TILE, a.shape[1]), lambda i: (i, 0))
        w_spec = pl.BlockSpec(w.shape, lambda i: (0, 0), pipeline_mode=pl.Buffered(1))
    else:
        a_spec = pl.BlockSpec((a.shape[0], ROW_TILE, a.shape[2]), lambda i: (0, i, 0))
        w_spec = pl.BlockSpec(w.shape, lambda i: (0, 0, 0), pipeline_mode=pl.Buffered(1))
    in_specs = [a_spec, w_spec, pl.BlockSpec((ROW_TILE, n), lambda i: (i, 0))]
    args = (a, w, h)
    body = _matmul_residual_kernel
    if norm_g is not None:
        in_specs.append(pl.BlockSpec((1, n), lambda i: (0, 0)))
        args += (norm_g.reshape(1, n),)
        body = _matmul_residual_norm_kernel
    return pl.pallas_call(
        body,
        grid=(rows // ROW_TILE,),
        in_specs=in_specs,
        out_specs=pl.BlockSpec((ROW_TILE, n), lambda i: (i, 0)),
        out_shape=jax.ShapeDtypeStruct((rows, n), F32),
        compiler_params=_params("arbitrary"),
        name="matmul_residual",
    )(*args)


def _gla_kernel(q_ref, f_ref, i_ref, g_ref, lb_ref, gain_ref, o_ref,
                st_ref, b_scr, k_scr, v_scr, *, layer, n_chunks):
    C, SUB, HALF = GLA_CHUNK, GLA_SUB, GLA_SUB // 2
    n_sub = C // SUB
    heads = range(GLA_HEADS_PER_STEP)
    lanes = [slice(h * HG_DK, (h + 1) * HG_DK) for h in heads]

    @pl.when(pl.program_id(2) == 0)
    def _():
        st_ref[...] = jnp.zeros_like(st_ref)

    lbp = lb_ref[...]
    e = jnp.exp(lbp - jnp.max(lbp, axis=0, keepdims=True))
    p = e / jnp.sum(e, axis=0, keepdims=True)
    lb_all = jnp.sum(p[:layer + 1], axis=0, keepdims=True) - p[0:1]
    gain_all = gain_ref[...]

    row = lax.broadcasted_iota(jnp.int32, (C, C), 0)
    col = lax.broadcasted_iota(jnp.int32, (C, C), 1)
    tril = (row >= col).astype(BF16)
    half_row = lax.broadcasted_iota(jnp.int32, (HALF, 1), 0)
    sub_col = lax.broadcasted_iota(jnp.int32, (SUB, C), 1)
    nt = (((1,), (1,)), ((), ()))

    def chunk(c, carry):
        r0 = pl.multiple_of(c * C, C)
        rows = pl.ds(r0, C)
        q, k, v16, b = [], [], [], []
        for h in heads:
            lb = lb_all[:, lanes[h]]
            qh = q_ref[rows, lanes[h]]
            q.append(qh * _sigmoid(qh))
            sig = _sigmoid(f_ref[rows, lanes[h]])
            lf = jnp.log(jnp.maximum(lb + (1.0 - lb) * sig, F_FLOOR))
            k.append((1.0 - lb) * (1.0 - sig))
            v = i_ref[rows, lanes[h]]
            v_scr[h] = v
            v16.append(v.astype(BF16))
            lf_hi = lf.astype(BF16)
            rem = lf - lf_hi.astype(F32)
            lf_mid = rem.astype(BF16)
            lf_lo = (rem - lf_mid.astype(F32)).astype(BF16)
            b.append(jnp.dot(tril, lf_hi, preferred_element_type=F32)
                     + jnp.dot(tril, lf_mid, preferred_element_type=F32)
                     + jnp.dot(tril, lf_lo, preferred_element_type=F32))

        st, o_inter, a_off, kv, decay = [], [], [], [], []
        for h in heads:
            b_scr[h] = b[h]
            k_scr[h] = k[h]
            b_last = b_scr[h, C - 1:C, :]
            st.append(st_ref[h])
            qb = (q[h] * jnp.exp(b[h])).astype(BF16)
            o_inter.append(lax.dot_general(qb, st[h].astype(BF16), nt,
                                           preferred_element_type=F32))
            kd = (k[h] * jnp.exp(b_last - b[h])).astype(BF16)
            kv.append(lax.dot_general(v16[h], kd, (((0,), (0,)), ((), ())),
                                      preferred_element_type=F32))
            decay.append(jnp.exp(b_last))
            a_rows = [jnp.zeros((SUB, C), F32)]
            for i in range(1, n_sub):
                lo = i * SUB
                r_i = b_scr[h, lo - 1:lo, :]
                q_s = (q[h][lo:lo + SUB] * jnp.exp(b[h][lo:lo + SUB] - r_i)).astype(BF16)
                k_s = (k[h] * jnp.exp(jnp.minimum(r_i - b[h], 0.0))).astype(BF16)
                a_i = lax.dot_general(q_s, k_s, nt, preferred_element_type=F32)
                a_rows.append(jnp.where(sub_col < lo, a_i, 0.0))
            a_off.append(jnp.concatenate(a_rows, axis=0).astype(BF16))

        o_diag = []
        for h in heads:
            pieces = []
            for i in range(n_sub):
                lo = i * SUB
                q_lo, q_hi = q[h][lo:lo + HALF], q[h][lo + HALF:lo + SUB]
                b_lo, b_hi = b[h][lo:lo + HALF], b[h][lo + HALF:lo + SUB]
                acc_lo = jnp.zeros((HALF, HG_DK), F32)
                acc_hi = jnp.zeros((HALF, HG_DK), F32)
                for s in range(SUB):
                    b_s = b_scr[h, lo + s:lo + s + 1, :]
                    k_row = k_scr[h, lo + s:lo + s + 1, :]
                    v_row = v_scr[h, lo + s:lo + s + 1, :]
                    w_hi = jnp.sum(q_hi * k_row * jnp.exp(b_hi - b_s), axis=-1, keepdims=True)
                    if s < HALF:
                        w_lo = jnp.sum(q_lo * k_row * jnp.exp(b_lo - b_s), axis=-1,
                                       keepdims=True)
                        acc_lo = acc_lo + jnp.where(half_row >= s, w_lo, 0.0) * v_row
                        acc_hi = acc_hi + w_hi * v_row
                    else:
                        acc_hi = acc_hi + jnp.where(half_row >= s - HALF, w_hi, 0.0) * v_row
                pieces += [acc_lo, acc_hi]
            o_diag.append(jnp.concatenate(pieces, axis=0))

        for h in heads:
            o = (o_inter[h] + jnp.dot(a_off[h], v16[h], preferred_element_type=F32)
                 + o_diag[h])
            st_ref[h] = st[h] * decay[h] + kv[h]
            g = g_ref[rows, lanes[h]]
            o = o * lax.rsqrt(jnp.mean(o * o, axis=-1, keepdims=True) + EPS)
            o_ref[rows, lanes[h]] = (o * gain_all[:, lanes[h]] * (g * _sigmoid(g))
                                     ).astype(o_ref.dtype)
        return carry

    lax.fori_loop(0, n_chunks, chunk, 0)


def gla_mixer(proj, hg_lb, gain, layer, batch, l_pad):
    rows = proj.shape[0]
    n_t = l_pad // ROW_TILE
    n_layers = hg_lb.shape[0]
    hps = GLA_HEADS_PER_STEP
    width = hps * HG_DK
    groups = HG_HEADS // hps
    col = lambda part: (lambda b, h, t: (b * n_t + t, part * groups + h))
    return pl.pallas_call(
        functools.partial(_gla_kernel, layer=layer, n_chunks=ROW_TILE // GLA_CHUNK),
        grid=(batch, groups, n_t),
        in_specs=[
            pl.BlockSpec((ROW_TILE, width), col(0)),
            pl.BlockSpec((ROW_TILE, width), col(1)),
            pl.BlockSpec((ROW_TILE, width), col(2)),
            pl.BlockSpec((ROW_TILE, width), col(3)),
            pl.BlockSpec((n_layers, width), lambda b, h, t: (0, h)),
            pl.BlockSpec((1, width), lambda b, h, t: (0, h)),
        ],
        out_specs=pl.BlockSpec((ROW_TILE, width), col(0)),
        out_shape=jax.ShapeDtypeStruct((rows, D_MODEL), BF16),
        scratch_shapes=[
            pltpu.VMEM((hps, HG_DK, HG_DK), F32),
            pltpu.VMEM((hps, GLA_CHUNK, HG_DK), F32),
            pltpu.VMEM((hps, GLA_CHUNK, HG_DK), F32),
            pltpu.VMEM((hps, GLA_CHUNK, HG_DK), F32),
        ],
        compiler_params=_params("arbitrary", "arbitrary", "arbitrary"),
        name="gla_mixer",
    )(proj, proj, proj, proj, hg_lb, gain.reshape(1, D_MODEL))


def _ffn_up_kernel(x_ref, g_ref, w_ref, cw_ref, cb_ref, o_ref, xs_ref, xn_ref, gate_ref,
                   carry_ref, *, tiles_per_seq):
    i = pl.program_id(0)
    tm = xn_ref.shape[0]
    rs = FFN_ROW_SUB
    n_cb = o_ref.shape[0]
    tn = o_ref.shape[-1]
    grp = SUBLANES
    win = grp * grp
    n_win = rs // win

    def interleave(ref, lead, base):
        return jnp.concatenate(
            [jnp.concatenate([ref[lead + (j, pl.ds(base + v, grp, stride=grp), slice(None))]
                              for j in range(ref.shape[len(lead)])], axis=1)
             for v in range(grp)], axis=0)

    for j in range(xs_ref.shape[0]):
        xs_ref[j] = x_ref[:, j * LANES:(j + 1) * LANES]

    def normalize_block(r):
        for w in range(r * n_win, (r + 1) * n_win):
            xn_ref[w * win:(w + 1) * win, :] = _rms_rows(
                interleave(xs_ref, (), w * win), g_ref[...]).astype(BF16)

    @pl.when((i % tiles_per_seq) == 0)
    def _():
        carry_ref[...] = jnp.zeros_like(carry_ref)

    sub_row = lax.broadcasted_iota(jnp.int32, (grp, 1), 0)

    def project(c, r):
        xr = xn_ref[r * rs:(r + 1) * rs, :]
        shape = (n_win, grp, grp, tn)
        return (jnp.dot(xr, w_ref[c], preferred_element_type=F32).reshape(shape),
                jnp.dot(xr, w_ref[n_cb + c], preferred_element_type=F32).reshape(shape))

    def wrapped(prev_g, cur_g):
        return jnp.where(sub_row == 0, pltpu.roll(prev_g, 1, axis=0),
                         pltpu.roll(cur_g, 1, axis=0))[None]

    def conv(prev, u, cb):
        cw = cw_ref[cb]
        outs = []
        for g in range(n_win):
            cur = u[g]
            before = prev if g == 0 else u[g - 1, grp - 2:]
            w1 = wrapped(before[1], cur[grp - 1])
            w2 = wrapped(before[0], cur[grp - 2])
            u1 = jnp.concatenate([w1, cur[:grp - 1]], axis=0)
            u2 = jnp.concatenate([w2, w1, cur[:grp - 2]], axis=0)
            outs.append((u2 * cw[0:1, :] + u1 * cw[1:2, :] + cur * cw[2:3, :]
                         + cb_ref[cb])[None])
        return jnp.concatenate(outs, axis=0)

    stages = [(c, r) for c in range(n_cb) for r in range(tm // rs)]
    normalize_block(0)
    u_next = project(*stages[0])
    prev_a = prev_b = None
    for idx, (c, r) in enumerate(stages):
        ua, ub = u_next
        if idx + 1 < len(stages):
            if stages[idx + 1][0] == 0:
                normalize_block(stages[idx + 1][1])
            u_next = project(*stages[idx + 1])
        if r == 0:
            prev_a, prev_b = carry_ref[c], carry_ref[n_cb + c]
        a = conv(prev_a, ua, c)
        b = conv(prev_b, ub, n_cb + c)
        prev_a, prev_b = ua[n_win - 1, grp - 2:], ub[n_win - 1, grp - 2:]
        if (r + 1) * rs == tm:
            carry_ref[c] = prev_a
            carry_ref[n_cb + c] = prev_b
        slot = idx % 2
        gated = (a * _sigmoid(a) * b).reshape(rs, tn)
        for j in range(gate_ref.shape[1]):
            gate_ref[slot, j] = gated[:, j * LANES:(j + 1) * LANES]
        natural = jnp.concatenate(
            [interleave(gate_ref, (slot,), w * win) for w in range(n_win)], axis=0)
        o_ref[c, r * rs:(r + 1) * rs, :] = natural.astype(o_ref.dtype)


def ffn_up(h, g, w_up, conv_w, conv_b, l_pad):
    rows, d = h.shape
    tn = FFN_COL_TILE
    n_cb = D_FF // tn
    w3 = w_up.reshape(d, 2 * n_cb, tn).transpose(1, 0, 2)
    cw3 = conv_w.reshape(CONV_W, 2 * n_cb, tn).transpose(1, 0, 2)
    cb3 = conv_b.reshape(2 * n_cb, 1, tn)
    whole = lambda a: pl.BlockSpec(a.shape, lambda i: (0,) * a.ndim,
                                   pipeline_mode=pl.Buffered(1))
    return pl.pallas_call(
        functools.partial(_ffn_up_kernel, tiles_per_seq=l_pad // ROW_TILE),
        grid=(rows // ROW_TILE,),
        in_specs=[
            pl.BlockSpec((ROW_TILE, d), lambda i: (i, 0)),
            pl.BlockSpec((1, d), lambda i: (0, 0)),
            whole(w3), whole(cw3), whole(cb3),
        ],
        out_specs=pl.BlockSpec((n_cb, ROW_TILE, tn), lambda i: (0, i, 0)),
        out_shape=jax.ShapeDtypeStruct((n_cb, rows, tn), BF16),
        scratch_shapes=[
            pltpu.VMEM((d // LANES, ROW_TILE, LANES), F32),
            pltpu.VMEM((ROW_TILE, d), BF16),
            pltpu.VMEM((2, tn // LANES, FFN_ROW_SUB, LANES), F32),
            pltpu.VMEM((2 * n_cb, 2, SUBLANES, tn), F32),
        ],
        compiler_params=_params("arbitrary"),
        name="ffn_up",
    )(h, g.reshape(1, d), w3, cw3, cb3)


def _diff_attn_kernel(q_ref, k_ref, vt_ref, lq1_ref, lk1_ref, lq2_ref, lk2_ref, gain_ref,
                      o_ref, qs_ref, m_ref, l_ref, acc_ref, kmax_ref, *, lam_init):
    i = pl.program_id(2)
    tq = q_ref.shape[0]
    tk = tq
    n_t = vt_ref.shape[0]
    sub = ATT_SUB
    per_map = tq // sub
    n_sub = 2 * per_map

    @pl.when(i == 0)
    def _():
        def body(t, mx):
            kb = k_ref[pl.ds(pl.multiple_of(t * tk, tk), tk), :].astype(F32)
            return jnp.maximum(mx, jnp.max(jnp.sum(kb * kb, axis=-1, keepdims=True)))
        kmax_ref[0] = lax.fori_loop(0, n_t, body, jnp.float32(0.0))

    q = q_ref[...]
    lane = lax.broadcasted_iota(jnp.int32, (sub, 2 * DA_DH), 1)
    for c in range(per_map):
        q_c = q[c * sub:(c + 1) * sub, :]
        qs_ref[c] = jnp.where(lane < DA_DH, q_c, jnp.zeros_like(q_c))
        qs_ref[per_map + c] = jnp.where(lane >= DA_DH, q_c, jnp.zeros_like(q_c))
    l_ref[...] = jnp.zeros_like(l_ref)
    acc_ref[...] = jnp.zeros_like(acc_ref)
    qf = q.astype(F32)
    qmax = jnp.max(jnp.sum(qf * qf, axis=-1, keepdims=True))
    no_shift = qmax * kmax_ref[0] <= ATT_NOSHIFT_LIMIT ** 2

    def kv_blocks(blocks, shifted):
        def n_keys(masked, c):
            return ((c % per_map) + 1) * sub if masked else tk

        def scores(j, masked, c):
            row0 = pl.multiple_of(j * tk, tk)
            return lax.dot_general(k_ref[pl.ds(row0, n_keys(masked, c)), :], qs_ref[c],
                                   (((1,), (1,)), ((), ())),
                                   preferred_element_type=F32)

        def softmax(masked, c, s):
            if masked:
                key = lax.broadcasted_iota(jnp.int32, s.shape, 0)
                qry = lax.broadcasted_iota(jnp.int32, s.shape, 1) + (c % per_map) * sub
                s = jnp.where(key <= qry, s, NEG_BIG)
            if not shifted:
                p = jnp.exp2(s)
                l_ref[c] = l_ref[c] + jnp.sum(p, axis=0, keepdims=True)
                return None, p.astype(BF16)
            m_prev = m_ref[c]
            m_new = jnp.maximum(m_prev, jnp.max(s, axis=0, keepdims=True))
            alpha = jnp.exp2(m_prev - m_new)
            p = jnp.exp2(s - m_new)
            l_ref[c] = alpha * l_ref[c] + jnp.sum(p, axis=0, keepdims=True)
            m_ref[c] = m_new
            return alpha, p.astype(BF16)

        stages = [(j, masked, c) for j, masked in blocks for c in range(n_sub)]
        s_q = [scores(*stages[0]), scores(*stages[1])]
        ap = softmax(*stages[0][1:], s_q.pop(0))
        for g, (j, masked, c) in enumerate(stages):
            if g + 2 < len(stages):
                s_q.append(scores(*stages[g + 2]))
            alpha, p = ap
            if g + 1 < len(stages):
                ap = softmax(*stages[g + 1][1:], s_q.pop(0))
            pv = jnp.dot(vt_ref[j, :, pl.ds(0, n_keys(masked, c))], p,
                         preferred_element_type=F32)
            acc_ref[c] = (acc_ref[c] if alpha is None else alpha * acc_ref[c]) + pv

    def all_blocks(shifted, per_iter):
        def body(t, carry):
            kv_blocks([(t * per_iter + u, False) for u in range(per_iter)], shifted)
            return carry
        lax.fori_loop(0, i // per_iter, body, 0)
        for rem in range(per_iter):
            @pl.when(i % per_iter == rem)
            def _():
                kv_blocks([(i - rem + u, False) for u in range(rem)] + [(i, True)], shifted)

    @pl.when(no_shift)
    def _():
        all_blocks(False, ATT_BLOCKS_PER_ITER)

    @pl.when(jnp.logical_not(no_shift))
    def _():
        m_ref[...] = jnp.full_like(m_ref, NEG_BIG)
        all_blocks(True, 1)

    lam = (jnp.exp(jnp.sum(lq1_ref[...] * lk1_ref[...], axis=-1, keepdims=True))
           - jnp.exp(jnp.sum(lq2_ref[...] * lk2_ref[...], axis=-1, keepdims=True))
           + lam_init)
    for c in range(per_map):
        o_t = (acc_ref[c] / l_ref[c]
               - lam * (acc_ref[per_map + c] / l_ref[per_map + c]))
        o = o_t.T
        o = o * lax.rsqrt(jnp.mean(o * o, axis=-1, keepdims=True) + EPS) * gain_ref[...]
        o_ref[c * sub:(c + 1) * sub, :] = (o * (1.0 - lam_init)).astype(o_ref.dtype)


def diff_attention_core(q, k, vt, lq1, lk1, lq2, lk2, sub_gain, lam_init, batch, l_pad):
    rows = q.shape[0]
    t = ROW_TILE
    n_t = l_pad // t
    n_sub = 2 * t // ATT_SUB
    vec = lambda a: a.reshape(1, -1)
    small = lambda n: pl.BlockSpec((1, n), lambda b, h, i: (0, 0))
    return pl.pallas_call(
        functools.partial(_diff_attn_kernel, lam_init=lam_init),
        grid=(batch, DA_HEADS, n_t),
        in_specs=[
            pl.BlockSpec((t, 2 * DA_DH), lambda b, h, i: (b * n_t + i, h)),
            pl.BlockSpec((l_pad, 2 * DA_DH), lambda b, h, i: (b, h)),
            pl.BlockSpec((n_t, DA_DV, t), lambda b, h, i: (b, h, 0)),
            small(DA_DH), small(DA_DH), small(DA_DH), small(DA_DH), small(DA_DV),
        ],
        out_specs=pl.BlockSpec((t, DA_DV), lambda b, h, i: (b * n_t + i, h)),
        out_shape=jax.ShapeDtypeStruct((rows, D_MODEL), BF16),
        scratch_shapes=[
            pltpu.VMEM((n_sub, ATT_SUB, 2 * DA_DH), BF16),
            pltpu.VMEM((n_sub, 1, ATT_SUB), F32),
            pltpu.VMEM((n_sub, 1, ATT_SUB), F32),
            pltpu.VMEM((n_sub, DA_DV, ATT_SUB), F32),
            pltpu.SMEM((1,), F32),
        ],
        compiler_params=_params("arbitrary", "arbitrary", "arbitrary"),
        name="diff_attention",
    )(q, k, vt, vec(lq1), vec(lk1), vec(lq2), vec(lk2), vec(sub_gain))


def kernel(x, meta_tokens, norm_mix, norm_ffn, hg_w_in, hg_lb, hg_gain_o, hg_w_o, kv_norm, w_kv, da_w_q, da_lam_q1, da_lam_k1, da_lam_q2, da_lam_k2, da_sub_gain, da_w_o, ffn_w_up, ffn_conv_w, ffn_conv_b, ffn_w_down, final_norm_g):
    batch, seq, d = x.shape
    n_a = hg_w_in.shape[0]
    depth = norm_mix.shape[0]
    l_real = N_META + seq
    l_pad = -(-l_real // L_ALIGN) * L_ALIGN

    meta = jnp.broadcast_to(meta_tokens[None].astype(x.dtype), (batch, N_META, d))
    tail = jnp.zeros((batch, l_pad - l_real, d), x.dtype)
    h = jnp.concatenate([meta, x, tail], axis=1).reshape(batch * l_pad, d)

    k_sh = vt_sh = None
    for l in range(depth):
        if l < n_a:
            proj = norm_matmul(h, norm_mix[l], hg_w_in[l].astype(BF16), F32)
            o = gla_mixer(proj, hg_lb, hg_gain_o[l], l, batch, l_pad)
            h = matmul_residual(o, hg_w_o[l].astype(BF16), h)
        else:
            j = l - n_a
            lam_init = 0.8 - 0.6 * math.exp(-0.3 * l)
            q = norm_matmul(h, norm_mix[l], da_w_q[j].astype(BF16), BF16,
                            scale=DA_DH ** -0.5 * LOG2E)
            o = diff_attention_core(q, k_sh, vt_sh, da_lam_q1[j], da_lam_k1[j], da_lam_q2[j],
                                    da_lam_k2[j], da_sub_gain[j], lam_init, batch, l_pad)
            h = matmul_residual(o, da_w_o[j].astype(BF16), h)
        gated = ffn_up(h, norm_ffn[l], ffn_w_up[l].astype(BF16), ffn_conv_w[l],
                       ffn_conv_b[l], l_pad)
        w_down = ffn_w_down[l].astype(BF16).reshape(gated.shape[0], gated.shape[2], d)
        h = matmul_residual(gated, w_down, h,
                            norm_g=final_norm_g if l == depth - 1 else None)
        if l == n_a - 1:
            k_sh, vt_sh = shared_kv_proj(h, kv_norm, w_kv[:, :DA_QK].astype(BF16),
                                         w_kv[:, DA_QK:].T.astype(BF16), BF16)

    return h.reshape(batch, l_pad, d)[:, N_META:l_real]
```
